```python
import jax, jax.numpy as jnp
from jax import lax
import numpy as np

D_MODEL = 1024
BATCH = 2
SEQ = 8192
DEPTH = 1
DEC_BATCH = 128
DEC_SEQ = 8
PAST_LEN = 8192
PAGE_SIZE = 128

MLA_HEADS = 8
QK_NOPE_DIM = 64
QK_ROPE_DIM = 32
V_HEAD_DIM = 64
KV_LORA_RANK = 256
Q_LORA_RANK = 768
SB_HEADS = 8
SB_HEAD_DIM = 64

MLA_WIDTH = MLA_HEADS * V_HEAD_DIM
SB_WIDTH = SB_HEADS * SB_HEAD_DIM
MLA_ROW = KV_LORA_RANK + QK_ROPE_DIM
MLA_SCALE = (QK_NOPE_DIM + QK_ROPE_DIM) ** -0.5
SB_SCALE = SB_HEAD_DIM ** -0.5
Q_BLOCK = 128
ROPE_BASE = 10000.0
RMS_EPS = 1e-6
IN_SIZES = (Q_LORA_RANK, KV_LORA_RANK, QK_ROPE_DIM, SB_WIDTH, SB_WIDTH, SB_WIDTH, MLA_WIDTH, SB_WIDTH, D_MODEL, D_MODEL)
IN_COLS = Q_LORA_RANK + KV_LORA_RANK + QK_ROPE_DIM + 3 * SB_WIDTH + MLA_WIDTH + SB_WIDTH + 2 * D_MODEL

kernel_name = 'hybrid_mla_stickbreaking_decode_step'


def _rmsnorm(x, w):
    xf = x.astype(jnp.float32)
    xf = xf * lax.rsqrt(jnp.mean(xf * xf, axis=-1, keepdims=True) + RMS_EPS)
    return (xf * w.astype(jnp.float32)).astype(x.dtype)


def _rope(x, pos):
    half = x.shape[-1] // 2
    inv_freq = ROPE_BASE ** (-jnp.arange(half, dtype=jnp.float32) / half)
    ang = pos.astype(jnp.float32)[:, None] * inv_freq[None, :]
    shape = (1, pos.shape[0]) + (1,) * (x.ndim - 3) + (half,)
    cos = jnp.cos(ang).reshape(shape)
    sin = jnp.sin(ang).reshape(shape)
    xf = x.astype(jnp.float32)
    x1, x2 = xf[..., :half], xf[..., half:]
    return jnp.concatenate([x1 * cos - x2 * sin, x2 * cos + x1 * sin], axis=-1).astype(x.dtype)


def _split_in(h):
    parts = []
    start = 0
    for size in IN_SIZES:
        parts.append(h[..., start:start + size])
        start += size
    return parts


def _project(x, pos, norm_w, w_in, mla_q_a_norm_w, mla_w_uq, mla_q_head_norm_w, mla_w_uk,
             mla_kv_a_norm_w, mla_k_rope_norm_w):
    b, t, _ = x.shape
    h = _rmsnorm(x, norm_w) @ w_in
    q_a, c_kv, k_r, sb_q, sb_k, sb_v, gate_mla, gate_sb, merge_mla, merge_sb = _split_in(h)
    q = (_rmsnorm(q_a, mla_q_a_norm_w) @ mla_w_uq).reshape(b, t, MLA_HEADS, QK_NOPE_DIM + QK_ROPE_DIM)
    q = _rmsnorm(q, mla_q_head_norm_w)
    q_lat = jnp.einsum('bthn,rhn->bthr', q[..., :QK_NOPE_DIM], mla_w_uk)
    q_rope = _rope(q[..., QK_NOPE_DIM:], pos)
    k_rope = _rope(_rmsnorm(k_r, mla_k_rope_norm_w), pos)
    mla_kv = jnp.concatenate([_rmsnorm(c_kv, mla_kv_a_norm_w), k_rope], axis=-1)
    heads = lambda a: a.reshape(b, t, SB_HEADS, SB_HEAD_DIM)
    gates = (gate_mla, gate_sb, merge_mla, merge_sb)
    return q_lat, q_rope, mla_kv, heads(sb_q), heads(sb_k), heads(sb_v), gates


def _mla_attend(q_lat, q_rope, mla_kv, q_pos, k_pos):
    c = mla_kv[..., :KV_LORA_RANK]
    k_rope = mla_kv[..., KV_LORA_RANK:]
    s = (jnp.einsum('bthr,bsr->bhts', q_lat, c, preferred_element_type=jnp.float32)
         + jnp.einsum('bthd,bsd->bhts', q_rope, k_rope, preferred_element_type=jnp.float32)) * MLA_SCALE
    mask = k_pos[None, :] <= q_pos[:, None]
    p = jax.nn.softmax(jnp.where(mask, s, -jnp.inf), axis=-1)
    return jnp.einsum('bhts,bsr->bthr', p.astype(c.dtype), c)


def _sb_attend(q, k, v, q_pos, k_pos):
    z = jnp.einsum('bthd,bshd->bhts', q, k, preferred_element_type=jnp.float32) * SB_SCALE
    mask = k_pos[None, :] < q_pos[:, None]
    log_keep = jnp.where(mask, jax.nn.log_sigmoid(-z), 0.0)
    log_after = lax.cumsum(log_keep, axis=3, reverse=True) - log_keep
    a = jnp.where(mask, jnp.exp(jax.nn.log_sigmoid(z) + log_after), 0.0)
    return jnp.einsum('bhts,bshd->bthd', a.astype(v.dtype), v)


def _finish(x, o_lat, o_sb, gates, mla_w_uv, w_o_mla, w_o_sb, w_out):
    b, t, _ = x.shape
    gate_mla, gate_sb, merge_mla, merge_sb = gates
    o_mla = jnp.einsum('bthr,rhv->bthv', o_lat, mla_w_uv).reshape(b, t, MLA_WIDTH) * jax.nn.silu(gate_mla)
    o_sb = o_sb.reshape(b, t, SB_WIDTH) * jax.nn.silu(gate_sb)
    merged = jax.nn.sigmoid(merge_mla) * (o_mla @ w_o_mla) + jax.nn.sigmoid(merge_sb) * (o_sb @ w_o_sb)
    return x + merged @ w_out


def _to_blocks(a):
    b, t = a.shape[:2]
    return jnp.moveaxis(a.reshape((b, t // Q_BLOCK, Q_BLOCK) + a.shape[2:]), 1, 0)


def _from_blocks(a):
    nb, b, qb = a.shape[:3]
    return jnp.moveaxis(a, 0, 1).reshape((b, nb * qb) + a.shape[3:])


def setup_inputs(seed: int = 0) -> dict:
    key = jax.random.key(seed)
    ks = jax.random.split(key, 20)
    n_pages = PAST_LEN // PAGE_SIZE
    n_used = DEC_BATCH * n_pages
    n_pool = n_used + n_used // 4
    nrm = lambda k, shape, scale: jax.random.normal(k, shape, jnp.float32) * scale
    gain = lambda k, n: 1.0 + 0.01 * jax.random.normal(k, (n,), jnp.float32)
    page_table = jax.random.permutation(ks[5], n_pool)[:n_used].reshape(DEC_BATCH, n_pages).astype(jnp.int32)
    return {
        'x_prompt': nrm(ks[0], (BATCH, SEQ, D_MODEL), 1.0),
        'x_sample': nrm(ks[1], (DEC_BATCH, DEC_SEQ, D_MODEL), 1.0),
        'cache_mla_kv': nrm(ks[2], (n_pool, PAGE_SIZE, MLA_ROW), 1.0),
        'cache_sb_k': nrm(ks[3], (n_pool, PAGE_SIZE, SB_HEADS, SB_HEAD_DIM), 1.0),
        'cache_sb_v': nrm(ks[4], (n_pool, PAGE_SIZE, SB_HEADS, SB_HEAD_DIM), 1.0),
        'page_table': page_table,
        'norm_w': gain(ks[6], D_MODEL),
        'w_in': nrm(ks[7], (D_MODEL, IN_COLS), D_MODEL ** -0.5),
        'mla_q_a_norm_w': gain(ks[8], Q_LORA_RANK),
        'mla_w_uq': nrm(ks[9], (Q_LORA_RANK, MLA_HEADS * (QK_NOPE_DIM + QK_ROPE_DIM)), Q_LORA_RANK ** -0.5),
        'mla_q_head_norm_w': gain(ks[10], QK_NOPE_DIM + QK_ROPE_DIM),
        'mla_w_uk': nrm(ks[11], (KV_LORA_RANK, MLA_HEADS, QK_NOPE_DIM), KV_LORA_RANK ** -0.5),
        'mla_kv_a_norm_w': gain(ks[12], KV_LORA_RANK),
        'mla_k_rope_norm_w': gain(ks[13], QK_ROPE_DIM),
        'mla_w_uv': nrm(ks[14], (KV_LORA_RANK, MLA_HEADS, V_HEAD_DIM), KV_LORA_RANK ** -0.5),
        'w_o_mla': nrm(ks[15], (MLA_WIDTH, D_MODEL), MLA_WIDTH ** -0.5),
        'w_o_sb': nrm(ks[16], (SB_WIDTH, D_MODEL), SB_WIDTH ** -0.5),
        'w_out': nrm(ks[17], (D_MODEL, D_MODEL), D_MODEL ** -0.5),
    }


def reference(x_prompt, x_sample, cache_mla_kv, cache_sb_k, cache_sb_v, page_table,
              norm_w, w_in, mla_q_a_norm_w, mla_w_uq, mla_q_head_norm_w, mla_w_uk,
              mla_kv_a_norm_w, mla_k_rope_norm_w, mla_w_uv, w_o_mla, w_o_sb, w_out):
    proj_w = (norm_w, w_in, mla_q_a_norm_w, mla_w_uq, mla_q_head_norm_w, mla_w_uk,
              mla_kv_a_norm_w, mla_k_rope_norm_w)
    out_w = (mla_w_uv, w_o_mla, w_o_sb, w_out)

    seq = x_prompt.shape[1]
    pos_p = jnp.arange(seq, dtype=jnp.int32)
    q_lat_p, q_rope_p, mla_kv_p, sbq_p, sbk_p, sbv_p, gates_p = _project(x_prompt, pos_p, *proj_w)

    def prompt_block(args):
        ql, qr, sq, qp = args
        return (_mla_attend(ql, qr, mla_kv_p, qp, pos_p), _sb_attend(sq, sbk_p, sbv_p, qp, pos_p))

    o_lat_b, o_sb_b = lax.map(prompt_block, (_to_blocks(q_lat_p), _to_blocks(q_rope_p),
                                             _to_blocks(sbq_p), pos_p.reshape(-1, Q_BLOCK)))
    y_prompt = _finish(x_prompt, _from_blocks(o_lat_b), _from_blocks(o_sb_b), gates_p, *out_w)

    n_dec, dec_seq = x_sample.shape[:2]
    past_len = page_table.shape[1] * PAGE_SIZE
    pos_s = past_len + jnp.arange(dec_seq, dtype=jnp.int32)
    q_lat_s, q_rope_s, mla_kv_s, sbq_s, sbk_s, sbv_s, gates_s = _project(x_sample, pos_s, *proj_w)
    k_pos = jnp.arange(past_len + dec_seq, dtype=jnp.int32)

    def gather(cache):
        return cache[page_table].reshape((n_dec, past_len) + cache.shape[2:])

    mla_keys = jnp.concatenate([gather(cache_mla_kv), mla_kv_s], axis=1)
    sb_keys = jnp.concatenate([gather(cache_sb_k), sbk_s], axis=1)
    sb_vals = jnp.concatenate([gather(cache_sb_v), sbv_s], axis=1)
    o_lat_s = _mla_attend(q_lat_s, q_rope_s, mla_keys, pos_s, k_pos)
    o_sb_s = _sb_attend(sbq_s, sb_keys, sb_vals, pos_s, k_pos)
    y_sample = _finish(x_sample, o_lat_s, o_sb_s, gates_s, *out_w)

    return (y_prompt, y_sample, mla_kv_p, sbk_p, sbv_p, mla_kv_s, sbk_s, sbv_s)
```

```python
import functools

import numpy as np
import jax
import jax.numpy as jnp
from jax import lax
from jax.experimental import pallas as pl
from jax.experimental.pallas import tpu as pltpu

F32 = jnp.float32
BF16 = jnp.bfloat16

D_MODEL = 1024
HEADS = 8
QK_NOPE = 64
QK_ROPE = 32
HALF_ROPE = QK_ROPE // 2
Q_HEAD = QK_NOPE + QK_ROPE
V_HEAD = 64
KV_RANK = 256
Q_RANK = 768
SB_DIM = 64
WIDTH = HEADS * V_HEAD
MLA_ROW = KV_RANK + QK_ROPE
PAGE = 128
ROPE_BASE = 10000.0
RMS_EPS = 1e-6
MLA_SCALE = Q_HEAD ** -0.5
SB_SCALE = SB_DIM ** -0.5

LANES = 128
QPAD = 3 * LANES
LOG2E = 1.4426950408889634
SB_DEAD = -104.0
VMEM_LIMIT = 56 * 1024 * 1024

C_QA = 0
C_CKV = C_QA + Q_RANK
C_SBQ = C_CKV + KV_RANK
C_SBK = C_SBQ + WIDTH
C_SBV = C_SBK + WIDTH
C_GMLA = C_SBV + WIDTH
C_GSB = C_GMLA + WIDTH
C_MMLA = C_GSB + WIDTH
C_MSB = C_MMLA + D_MODEL
C_KR = C_MSB + D_MODEL
C_END = C_KR + LANES


def _rms(x, n):
    return x * lax.rsqrt(jnp.sum(x * x, axis=-1, keepdims=True) * (1.0 / n) + RMS_EPS)


def _sigmoid(x):
    return 1.0 / (1.0 + jnp.exp(-x))


def _split_bf16(x):
    hi = x.astype(BF16)
    lo = (x - hi.astype(F32)).astype(BF16)
    return hi, lo


def _dot(a, b):
    return jnp.dot(a, b, preferred_element_type=F32)


def _dot_nt(a, b):
    return lax.dot_general(a, b, (((1,), (1,)), ((), ())), preferred_element_type=F32)


def _proj_kernel(x_ref, cq_ref, sq_ref, ck_ref, sk_ref, normw_ref, w1_ref, qanw_ref,
                 wuq_ref, gq_ref, hsel_ref, wuk_ref, perm_ref, kvnw_ref, krnw_ref,
                 mlakv_ref, sbk_ref, sbv_ref, qmla_ref, kvbf_ref, sbq_ref, sbkbf_ref,
                 sbvbf_ref, gmla_ref, gsb_ref, mmla_ref, msb_ref):
    x = x_ref[...]
    xn = (_rms(x, D_MODEL) * normw_ref[...]).astype(BF16)

    def proj(a, b):
        return _dot(xn, w1_ref[:, a:b])

    q_a = proj(C_QA, C_CKV)
    qan = (_rms(q_a, Q_RANK) * qanw_ref[...]).astype(BF16)
    q = _dot(qan, wuq_ref[...])
    hi, lo = _split_bf16(q * q)
    ssq = _dot(hi, hsel_ref[...]) + _dot(lo, hsel_ref[...])
    qn = q * lax.rsqrt(ssq * (1.0 / Q_HEAD) + RMS_EPS) * gq_ref[...]
    x1 = qn[:, HEADS * QK_NOPE:HEADS * QK_NOPE + LANES]
    x2 = qn[:, HEADS * QK_NOPE + LANES:]
    cq = cq_ref[...]
    sq = sq_ref[...]
    rot = jnp.concatenate([x1 * cq - x2 * sq, x2 * cq + x1 * sq], axis=1).astype(BF16)
    rope = _dot(rot, perm_ref[...]).astype(BF16)
    for h in range(HEADS):
        pair = qn[:, LANES * (h // 2):LANES * (h // 2 + 1)].astype(BF16)
        q_lat = _dot(pair, wuk_ref[h])
        qmla_ref[:, QPAD * h:QPAD * h + KV_RANK] = q_lat.astype(BF16)
        qmla_ref[:, QPAD * h + KV_RANK:QPAD * (h + 1)] = rope[:, LANES * h:LANES * (h + 1)]

    cn = _rms(proj(C_CKV, C_SBQ), KV_RANK) * kvnw_ref[...]
    k_r = proj(C_KR, C_END)
    krn = _rms(k_r, QK_ROPE) * krnw_ref[...]
    lane = lax.broadcasted_iota(jnp.int32, krn.shape, 1)
    swapped = jnp.where(lane < HALF_ROPE, pltpu.roll(krn, LANES - HALF_ROPE, 1), pltpu.roll(krn, HALF_ROPE, 1))
    kro = krn * ck_ref[...] + swapped * sk_ref[...]
    mlakv_ref[:, :KV_RANK] = cn
    mlakv_ref[:, KV_RANK:] = kro[:, :QK_ROPE]
    kvbf_ref[:, :KV_RANK] = cn.astype(BF16)
    kvbf_ref[:, KV_RANK:] = kro.astype(BF16)

    sbq_ref[...] = (proj(C_SBQ, C_SBK) * SB_SCALE).astype(BF16)
    sbk = proj(C_SBK, C_SBV)
    sbk_ref[...] = sbk
    sbkbf_ref[...] = sbk.astype(BF16)
    sbv = proj(C_SBV, C_GMLA)
    sbv_ref[...] = sbv
    sbvbf_ref[...] = sbv.astype(BF16)

    g = proj(C_GMLA, C_GSB)
    gmla_ref[...] = g * _sigmoid(g)
    g = proj(C_GSB, C_MMLA)
    gsb_ref[...] = g * _sigmoid(g)
    mmla_ref[...] = _sigmoid(proj(C_MMLA, C_MSB))
    msb_ref[...] = _sigmoid(proj(C_MSB, C_KR))


def _const_spec(shape):
    nd = len(shape)
    return pl.BlockSpec(shape, lambda *_: (0,) * nd, pipeline_mode=pl.Buffered(1))


def _proj(x, tables, table_map, weights, tm):
    n = x.shape[0]
    row = lambda w: pl.BlockSpec((tm, w), lambda i: (i, 0))
    tab = pl.BlockSpec((tm, LANES), table_map)
    widths = [(MLA_ROW, F32), (WIDTH, F32), (WIDTH, F32), (HEADS * QPAD, BF16), (QPAD, BF16),
              (WIDTH, BF16), (WIDTH, BF16), (WIDTH, BF16), (WIDTH, F32), (WIDTH, F32),
              (D_MODEL, F32), (D_MODEL, F32)]
    return pl.pallas_call(
        _proj_kernel,
        grid=(n // tm,),
        in_specs=[row(D_MODEL), tab, tab, tab, tab] + [_const_spec(w.shape) for w in weights],
        out_specs=[row(w) for w, _ in widths],
        out_shape=[jax.ShapeDtypeStruct((n, w), dt) for w, dt in widths],
        compiler_params=pltpu.CompilerParams(dimension_semantics=("arbitrary",), vmem_limit_bytes=VMEM_LIMIT),
        name="proj",
    )(x, *tables, *weights)


def _mla_prompt_kernel(q_ref, kv_ref, wuv_ref, o_ref, m_scr, l_scr, acc_scr, *, tq):
    i = pl.program_id(1)
    rows = HEADS * tq
    q = q_ref[...]
    qs = jnp.concatenate([q[:, QPAD * h:QPAD * (h + 1)] for h in range(HEADS)], axis=0)
    m_scr[...] = jnp.full(m_scr.shape, -jnp.inf, F32)
    l_scr[...] = jnp.zeros(l_scr.shape, F32)
    acc_scr[...] = jnp.zeros(acc_scr.shape, F32)
    c = MLA_SCALE * LOG2E

    def step(j, diagonal):
        k = kv_ref[pl.ds(pl.multiple_of(j * tq, tq), tq), :]
        s = _dot_nt(qs, k)
        if diagonal:
            t = lax.broadcasted_iota(jnp.int32, s.shape, 0) & (tq - 1)
            col = lax.broadcasted_iota(jnp.int32, s.shape, 1)
            s = jnp.where(col <= t, s, -jnp.inf)
        m_prev = m_scr[...]
        m_new = jnp.maximum(m_prev, jnp.max(s, axis=-1, keepdims=True))
        alpha = jnp.exp2((m_prev - m_new) * c)
        p = jnp.exp2((s - jnp.tile(m_new, (1, tq // LANES))) * c)
        l_scr[...] = alpha * l_scr[...] + jnp.sum(p, axis=-1, keepdims=True)
        acc_scr[...] = jnp.tile(alpha, (1, KV_RANK // LANES)) * acc_scr[...] + _dot(p.astype(BF16), k[:, :KV_RANK])
        m_scr[...] = m_new

    def body(j, carry):
        step(j, False)
        return carry

    lax.fori_loop(0, i, body, 0)
    step(i, True)

    o_lat = (acc_scr[...] * jnp.tile(1.0 / l_scr[...], (1, KV_RANK // LANES))).astype(BF16)
    out = _dot(o_lat[:tq], wuv_ref[0])
    for h in range(1, HEADS):
        out += _dot(o_lat[h * tq:(h + 1) * tq], wuv_ref[h])
    o_ref[...] = out


def _mla_prompt(qmla, kvbf, wuv, batch, seq, tq):
    nq = seq // tq
    rows = HEADS * tq
    return pl.pallas_call(
        functools.partial(_mla_prompt_kernel, tq=tq),
        grid=(batch, nq),
        in_specs=[pl.BlockSpec((tq, HEADS * QPAD), lambda b, i: (b * nq + i, 0)),
                  pl.BlockSpec((seq, QPAD), lambda b, i: (b, 0)),
                  _const_spec(wuv.shape)],
        out_specs=pl.BlockSpec((tq, WIDTH), lambda b, i: (b * nq + i, 0)),
        out_shape=jax.ShapeDtypeStruct((batch * seq, WIDTH), F32),
        scratch_shapes=[pltpu.VMEM((rows, LANES), F32), pltpu.VMEM((rows, LANES), F32),
                        pltpu.VMEM((rows, KV_RANK), F32)],
        compiler_params=pltpu.CompilerParams(dimension_semantics=("arbitrary", "arbitrary"),
                                             vmem_limit_bytes=VMEM_LIMIT),
        name="mla_prompt",
    )(qmla, kvbf, wuv)


def _softplus(z):
    return jnp.maximum(z, 0.0) + jnp.log1p(jnp.exp(-jnp.abs(z)))


def _sb_block(z, v, carry, tri, mask):
    lk = -_softplus(z)
    if mask is not None:
        lk = jnp.where(mask, lk, 0.0)
    hi, lo = _split_bf16(lk)
    lo2 = (lk - hi.astype(F32) - lo.astype(F32)).astype(BF16)
    csum = _dot(hi, tri) + _dot(lo, tri) + _dot(lo2, tri)
    a = jnp.exp(z + csum + jnp.tile(carry, (1, z.shape[1] // LANES)))
    if mask is not None:
        a = jnp.where(mask, a, 0.0)
    return _dot(a.astype(BF16), v), carry + jnp.sum(lk, axis=-1, keepdims=True)


def _sb_prompt_kernel(q_ref, k_ref, v_ref, tri_ref, o_ref, carry_scr, acc_scr, *, tq):
    i = pl.program_id(1)
    q = q_ref[...]
    lane = lax.broadcasted_iota(jnp.int32, (tq, LANES), 1)
    zero = jnp.zeros((tq, LANES), BF16)
    qh = []
    for h in range(HEADS):
        pair = q[:, LANES * (h // 2):LANES * (h // 2 + 1)]
        qh.append(jnp.where((lane >= SB_DIM) if h % 2 else (lane < SB_DIM), pair, zero))
    carry_scr[...] = jnp.zeros(carry_scr.shape, F32)
    acc_scr[...] = jnp.zeros(acc_scr.shape, F32)
    tri = tri_ref[...]

    def block(j, mask):
        start = pl.multiple_of(j * tq, tq)
        for h in range(HEADS):
            g = h // 2
            k = k_ref[pl.ds(start, tq), LANES * g:LANES * (g + 1)]
            v = v_ref[pl.ds(start, tq), LANES * g:LANES * (g + 1)]
            z = _dot_nt(qh[h], k)
            out, carry = _sb_block(z, v, carry_scr[h], tri, mask)
            acc_scr[h] += out
            carry_scr[h] = carry

    row = lax.broadcasted_iota(jnp.int32, (tq, tq), 0)
    col = lax.broadcasted_iota(jnp.int32, (tq, tq), 1)
    block(i, col < row)

    def alive():
        return (jnp.max(carry_scr[...]) >= SB_DEAD).astype(jnp.int32)

    def cond(state):
        j, live = state
        return jnp.logical_and(j >= 0, live == 1)

    def body(state):
        j, _ = state
        block(j, None)
        return j - 1, alive()

    lax.while_loop(cond, body, (i - 1, alive()))

    for g in range(HEADS // 2):
        o_ref[:, LANES * g:LANES * (g + 1)] = jnp.where(lane < SB_DIM, acc_scr[2 * g], acc_scr[2 * g + 1])


def _sb_prompt(sbq, sbk, sbv, tri, batch, seq, tq):
    nq = seq // tq
    kv_spec = pl.BlockSpec((seq, WIDTH), lambda b, i: (b, 0), pipeline_mode=pl.Buffered(1))
    return pl.pallas_call(
        functools.partial(_sb_prompt_kernel, tq=tq),
        grid=(batch, nq),
        in_specs=[pl.BlockSpec((tq, WIDTH), lambda b, i: (b * nq + i, 0)), kv_spec, kv_spec,
                  _const_spec(tri.shape)],
        out_specs=pl.BlockSpec((tq, WIDTH), lambda b, i: (b * nq + i, 0)),
        out_shape=jax.ShapeDtypeStruct((batch * seq, WIDTH), F32),
        scratch_shapes=[pltpu.VMEM((HEADS, tq, LANES), F32), pltpu.VMEM((HEADS, tq, LANES), F32)],
        compiler_params=pltpu.CompilerParams(dimension_semantics=("arbitrary", "arbitrary"),
                                             vmem_limit_bytes=VMEM_LIMIT),
        name="sb_prompt",
    )(sbq, sbk, sbv, tri)


SB_PREFETCH = 2
MLA_CHUNK = 1024


def _decode_kernel(pt_ref, q_ref, kvnew_ref, sbq_ref, sbknew_ref, sbvnew_ref, wuv_ref, tri_ref,
                   cmla_hbm, csbk_hbm, csbv_hbm, omla_ref, osb_ref,
                   kbuf, sbkbuf, sbvbuf, slowk, slowv, carry_scr, acc_scr,
                   sem_mla, sem_sbk, sem_sbv, sem_slow, *, n_seq, n_pages, dec):
    n = pl.program_id(0)
    slot = n & 1
    past = n_pages * PAGE
    rows = HEADS * dec

    def mla_copy(seq, s, p):
        return pltpu.make_async_copy(cmla_hbm.at[pt_ref[seq, p]], kbuf.at[s, pl.ds(p * PAGE, PAGE)], sem_mla.at[s])

    def sb_copy(seq, s, r, src, dst, sem):
        return pltpu.make_async_copy(src.at[pt_ref[seq, n_pages - 1 - r]], dst.at[s, pl.ds(r * PAGE, PAGE)], sem.at[s])

    def fetch(seq, s):
        def issue(p, c):
            mla_copy(seq, s, p).start()
            return c
        lax.fori_loop(0, n_pages, issue, 0)
        for r in range(SB_PREFETCH):
            sb_copy(seq, s, r, csbk_hbm, sbkbuf, sem_sbk).start()
            sb_copy(seq, s, r, csbv_hbm, sbvbuf, sem_sbv).start()

    @pl.when(n == 0)
    def _():
        for s in range(2):
            kbuf[s, past:, :] = jnp.zeros((PAGE, MLA_ROW), F32)
        fetch(0, 0)

    @pl.when(n + 1 < n_seq)
    def _():
        fetch(n + 1, 1 - slot)

    lane = lax.broadcasted_iota(jnp.int32, (rows, WIDTH), 1)
    rowi = lax.broadcasted_iota(jnp.int32, (rows, WIDTH), 0)
    own = (lane >> 6) == (rowi >> 3)
    qbd = jnp.where(own, jnp.tile(sbq_ref[0].astype(F32), (HEADS, 1)), 0.0).astype(BF16)
    tri = tri_ref[...]
    pad = jnp.zeros((PAGE - dec, WIDTH), F32)
    knew = jnp.concatenate([sbknew_ref[0], pad], axis=0).astype(BF16)
    vnew = jnp.concatenate([sbvnew_ref[0], pad], axis=0).astype(BF16)
    t_of_row = lax.broadcasted_iota(jnp.int32, (rows, PAGE), 0) & (dec - 1)
    col = lax.broadcasted_iota(jnp.int32, (rows, PAGE), 1)
    out, carry = _sb_block(_dot_nt(qbd, knew), vnew, jnp.zeros((rows, LANES), F32), tri, col < t_of_row)
    acc_scr[...] = out
    carry_scr[...] = carry

    def sb_page(kpage, vpage):
        o, c = _sb_block(_dot_nt(qbd, kpage.astype(BF16)), vpage.astype(BF16), carry_scr[...], tri, None)
        acc_scr[...] += o
        carry_scr[...] = c

    for r in range(SB_PREFETCH):
        sb_copy(n, slot, r, csbk_hbm, sbkbuf, sem_sbk).wait()
        sb_copy(n, slot, r, csbv_hbm, sbvbuf, sem_sbv).wait()
    for r in range(SB_PREFETCH):
        sb_page(sbkbuf[slot, r * PAGE:(r + 1) * PAGE, :], sbvbuf[slot, r * PAGE:(r + 1) * PAGE, :])

    def alive():
        return (jnp.max(carry_scr[...]) >= SB_DEAD).astype(jnp.int32)

    def cond(state):
        p, live = state
        return jnp.logical_and(p >= 0, live == 1)

    def body(state):
        p, _ = state
        ck = pltpu.make_async_copy(csbk_hbm.at[pt_ref[n, p]], slowk, sem_slow.at[0])
        cv = pltpu.make_async_copy(csbv_hbm.at[pt_ref[n, p]], slowv, sem_slow.at[1])
        ck.start()
        cv.start()
        ck.wait()
        cv.wait()
        sb_page(slowk[...], slowv[...])
        return p - 1, alive()

    lax.while_loop(cond, body, (jnp.int32(n_pages - 1 - SB_PREFETCH), alive()))

    acc = acc_scr[...]
    sel = jnp.where(own, acc, 0.0)
    osb = sel[:dec]
    for h in range(1, HEADS):
        osb += sel[h * dec:(h + 1) * dec]
    osb_ref[0] = osb

    qf = q_ref[0].astype(F32)
    qs = jnp.concatenate([qf[:, QPAD * h:QPAD * h + MLA_ROW] for h in range(HEADS)], axis=0).astype(BF16)
    kbuf[slot, past:past + dec, :] = kvnew_ref[0]

    def wait_page(p, c):
        mla_copy(n, slot, p).wait()
        return c
    lax.fori_loop(0, n_pages, wait_page, 0)

    c = MLA_SCALE * LOG2E
    k = kbuf[slot, past:, :].astype(BF16)
    s = jnp.where(col <= t_of_row, _dot_nt(qs, k), -jnp.inf)
    m0 = jnp.max(s, axis=-1, keepdims=True)
    p0 = jnp.exp2((s - m0) * c)
    l0 = jnp.sum(p0, axis=-1, keepdims=True)
    a0 = _dot(p0.astype(BF16), k[:, :KV_RANK])

    def chunk(ci, state):
        m_prev, l_prev, acc_prev = state
        kc = kbuf[slot, pl.ds(pl.multiple_of(ci * MLA_CHUNK, MLA_CHUNK), MLA_CHUNK), :].astype(BF16)
        sc = _dot_nt(qs, kc)
        m_new = jnp.maximum(m_prev, jnp.max(sc, axis=-1, keepdims=True))
        alpha = jnp.exp2((m_prev - m_new) * c)
        pc = jnp.exp2((sc - m_new) * c)
        l_new = alpha * l_prev + jnp.sum(pc, axis=-1, keepdims=True)
        acc_new = alpha * acc_prev + _dot(pc.astype(BF16), kc[:, :KV_RANK])
        return m_new, l_new, acc_new

    _, l_fin, acc_fin = lax.fori_loop(0, past // MLA_CHUNK, chunk, (m0, l0, a0))
    o_lat = (acc_fin * (1.0 / l_fin)).astype(BF16)
    omla = _dot(o_lat[:dec], wuv_ref[0])
    for h in range(1, HEADS):
        omla += _dot(o_lat[h * dec:(h + 1) * dec], wuv_ref[h])
    omla_ref[0] = omla


def _decode(page_table, qmla, kvnew, sbq, sbknew, sbvnew, wuv, tri, cache_mla, cache_sbk, cache_sbv):
    n_seq, n_pages = page_table.shape
    dec = qmla.shape[1]
    rows = HEADS * dec
    per_seq = lambda w: pl.BlockSpec((1, dec, w), lambda n, pt: (n, 0, 0))
    const = lambda shape: pl.BlockSpec(shape, lambda n, pt: (0,) * len(shape), pipeline_mode=pl.Buffered(1))
    hbm = pl.BlockSpec(memory_space=pl.ANY)
    grid_spec = pltpu.PrefetchScalarGridSpec(
        num_scalar_prefetch=1,
        grid=(n_seq,),
        in_specs=[per_seq(HEADS * QPAD), per_seq(MLA_ROW), per_seq(WIDTH), per_seq(WIDTH), per_seq(WIDTH),
                  const(wuv.shape), const(tri.shape), hbm, hbm, hbm],
        out_specs=[per_seq(WIDTH), per_seq(WIDTH)],
        scratch_shapes=[
            pltpu.VMEM((2, n_pages * PAGE + PAGE, MLA_ROW), F32),
            pltpu.VMEM((2, SB_PREFETCH * PAGE, WIDTH), F32),
            pltpu.VMEM((2, SB_PREFETCH * PAGE, WIDTH), F32),
            pltpu.VMEM((PAGE, WIDTH), F32),
            pltpu.VMEM((PAGE, WIDTH), F32),
            pltpu.VMEM((rows, LANES), F32),
            pltpu.VMEM((rows, WIDTH), F32),
            pltpu.SemaphoreType.DMA((2,)),
            pltpu.SemaphoreType.DMA((2,)),
            pltpu.SemaphoreType.DMA((2,)),
            pltpu.SemaphoreType.DMA((2,)),
        ],
    )
    return pl.pallas_call(
        functools.partial(_decode_kernel, n_seq=n_seq, n_pages=n_pages, dec=dec),
        grid_spec=grid_spec,
        out_shape=[jax.ShapeDtypeStruct((n_seq, dec, WIDTH), F32)] * 2,
        compiler_params=pltpu.CompilerParams(dimension_semantics=("arbitrary",), vmem_limit_bytes=VMEM_LIMIT),
        name="decode",
    )(page_table, qmla, kvnew, sbq, sbknew, sbvnew, wuv, tri, cache_mla, cache_sbk, cache_sbv)


def _finish_kernel(x_ref, omla_ref, osb_ref, gmla_ref, gsb_ref, mmla_ref, msb_ref,
                   womla_ref, wosb_ref, wout_ref, y_ref):
    a = (omla_ref[...] * gmla_ref[...]).astype(BF16)
    b = (osb_ref[...] * gsb_ref[...]).astype(BF16)
    merged = mmla_ref[...] * _dot(a, womla_ref[...]) + msb_ref[...] * _dot(b, wosb_ref[...])
    y_ref[...] = x_ref[...] + _dot(merged.astype(BF16), wout_ref[...])


def _finish(x, omla, osb, gmla, gsb, mmla, msb, womla, wosb, wout, tm):
    n = x.shape[0]
    row = lambda w: pl.BlockSpec((tm, w), lambda i: (i, 0))
    return pl.pallas_call(
        _finish_kernel,
        grid=(n // tm,),
        in_specs=[row(D_MODEL), row(WIDTH), row(WIDTH), row(WIDTH), row(WIDTH), row(D_MODEL), row(D_MODEL),
                  _const_spec(womla.shape), _const_spec(wosb.shape), _const_spec(wout.shape)],
        out_specs=row(D_MODEL),
        out_shape=jax.ShapeDtypeStruct((n, D_MODEL), F32),
        compiler_params=pltpu.CompilerParams(dimension_semantics=("arbitrary",), vmem_limit_bytes=VMEM_LIMIT),
        name="finish",
    )(x, omla, osb, gmla, gsb, mmla, msb, womla, wosb, wout)


def _rope_tables(pos):
    inv_freq = ROPE_BASE ** (-jnp.arange(HALF_ROPE, dtype=F32) / HALF_ROPE)
    ang = pos.astype(F32)[:, None] * inv_freq[None, :]
    cos, sin = jnp.cos(ang), jnp.sin(ang)
    zeros = jnp.zeros((pos.shape[0], LANES - QK_ROPE), F32)
    return (jnp.tile(cos, (1, HEADS)), jnp.tile(sin, (1, HEADS)),
            jnp.concatenate([cos, cos, zeros], axis=1), jnp.concatenate([-sin, sin, zeros], axis=1))


def _head_select():
    head = np.concatenate([np.arange(HEADS * QK_NOPE) // QK_NOPE,
                           np.arange(LANES) // HALF_ROPE, np.arange(LANES) // HALF_ROPE])
    return jnp.asarray(head[:, None] == head[None, :], BF16)


def _rope_perm():
    p = np.zeros((2 * LANES, HEADS * LANES), np.float32)
    for h in range(HEADS):
        for d in range(HALF_ROPE):
            p[h * HALF_ROPE + d, h * LANES + d] = 1.0
            p[LANES + h * HALF_ROPE + d, h * LANES + HALF_ROPE + d] = 1.0
    return jnp.asarray(p, BF16)


def _pack_weights(norm_w, w_in, mla_q_a_norm_w, mla_w_uq, mla_q_head_norm_w, mla_w_uk,
                  mla_kv_a_norm_w, mla_k_rope_norm_w):
    sizes = (Q_RANK, KV_RANK, QK_ROPE, WIDTH, WIDTH, WIDTH, WIDTH, WIDTH, D_MODEL, D_MODEL)
    parts, start = [], 0
    for size in sizes:
        parts.append(w_in[:, start:start + size])
        start += size
    q_a, c_kv, k_r, sb_q, sb_k, sb_v, g_mla, g_sb, m_mla, m_sb = parts
    w1 = jnp.concatenate([q_a, c_kv, sb_q, sb_k, sb_v, g_mla, g_sb, m_mla, m_sb, k_r,
                          jnp.zeros((D_MODEL, LANES - QK_ROPE), F32)], axis=1).astype(BF16)
    wuq = mla_w_uq.reshape(Q_RANK, HEADS, Q_HEAD)
    wuq = jnp.concatenate([wuq[:, :, :QK_NOPE].reshape(Q_RANK, -1),
                           wuq[:, :, QK_NOPE:QK_NOPE + HALF_ROPE].reshape(Q_RANK, -1),
                           wuq[:, :, QK_NOPE + HALF_ROPE:].reshape(Q_RANK, -1)], axis=1).astype(BF16)
    g = mla_q_head_norm_w
    gq = jnp.concatenate([jnp.tile(g[:QK_NOPE], HEADS), jnp.tile(g[QK_NOPE:QK_NOPE + HALF_ROPE], HEADS),
                          jnp.tile(g[QK_NOPE + HALF_ROPE:], HEADS)])[None, :]
    wuk_t = jnp.transpose(mla_w_uk, (1, 2, 0))
    zeros = jnp.zeros_like(wuk_t)
    odd = (jnp.arange(HEADS) % 2 == 1)[:, None, None]
    wuk = jnp.concatenate([jnp.where(odd, zeros, wuk_t), jnp.where(odd, wuk_t, zeros)], axis=1).astype(BF16)
    krnw = jnp.concatenate([mla_k_rope_norm_w, jnp.zeros((LANES - QK_ROPE,), F32)])[None, :]
    return [norm_w[None, :], w1, mla_q_a_norm_w[None, :], wuq, gq, _head_select(), wuk, _rope_perm(),
            mla_kv_a_norm_w[None, :], krnw]


def _pack_wuv(mla_w_uv):
    tiled = jnp.tile(jnp.transpose(mla_w_uv, (1, 0, 2)), (1, 1, HEADS))
    own = (jnp.arange(WIDTH) // V_HEAD)[None, None, :] == jnp.arange(HEADS)[:, None, None]
    return jnp.where(own, tiled, 0.0).astype(BF16)


def _tile(n, cap):
    t = cap
    while n % t:
        t //= 2
    return t


def kernel(x_prompt, x_sample, cache_mla_kv, cache_sb_k, cache_sb_v, page_table, norm_w, w_in,
           mla_q_a_norm_w, mla_w_uq, mla_q_head_norm_w, mla_w_uk, mla_kv_a_norm_w, mla_k_rope_norm_w,
           mla_w_uv, w_o_mla, w_o_sb, w_out):
    batch, seq, _ = x_prompt.shape
    n_seq, dec, _ = x_sample.shape
    n_pages = page_table.shape[1]
    past = n_pages * PAGE
    assert past % MLA_CHUNK == 0 and n_pages > SB_PREFETCH and dec == 8

    weights = _pack_weights(norm_w, w_in, mla_q_a_norm_w, mla_w_uq, mla_q_head_norm_w, mla_w_uk,
                            mla_kv_a_norm_w, mla_k_rope_norm_w)
    wuv = _pack_wuv(mla_w_uv)
    womla, wosb, wout = w_o_mla.astype(BF16), w_o_sb.astype(BF16), w_out.astype(BF16)

    tq = _tile(seq, 256)
    tm = _tile(seq, 256)
    xp = x_prompt.reshape(batch * seq, D_MODEL)
    tabs = _rope_tables(jnp.arange(seq, dtype=jnp.int32))
    ntile = seq // tm
    (mlakv_p, sbk_p, sbv_p, qmla, kvbf, sbq, sbkbf, sbvbf, gmla, gsb, mmla, msb) = _proj(
        xp, tabs, lambda i: (i % ntile, 0), weights, tm)
    tri = jnp.asarray(np.tril(np.ones((tq, tq), np.float32)), BF16)
    omla = _mla_prompt(qmla, kvbf, wuv, batch, seq, tq)
    osb = _sb_prompt(sbq, sbkbf, sbvbf, tri, batch, seq, tq)
    y_prompt = _finish(xp, omla, osb, gmla, gsb, mmla, msb, womla, wosb, wout, tm)

    ns = n_seq * dec
    tms = _tile(ns, 256)
    xs = x_sample.reshape(ns, D_MODEL)
    tabs_s = [jnp.tile(t, (tms // dec, 1)) for t in _rope_tables(past + jnp.arange(dec, dtype=jnp.int32))]
    (mlakv_s, sbk_s, sbv_s, qmla_s, _, sbq_s, _, _, gmla_s, gsb_s, mmla_s, msb_s) = _proj(
        xs, tabs_s, lambda i: (0, 0), weights, tms)
    tri_s = jnp.asarray(np.tril(np.ones((PAGE, PAGE), np.float32)), BF16)
    per_seq = lambda a: a.reshape(n_seq, dec, a.shape[-1])
    omla_s, osb_s = _decode(page_table, per_seq(qmla_s), per_seq(mlakv_s), per_seq(sbq_s), per_seq(sbk_s),
                            per_seq(sbv_s), wuv, tri_s, cache_mla_kv,
                            cache_sb_k.reshape(cache_sb_k.shape[0], PAGE, WIDTH),
                            cache_sb_v.reshape(cache_sb_v.shape[0], PAGE, WIDTH))
    y_sample = _finish(xs, omla_s.reshape(ns, WIDTH), osb_s.reshape(ns, WIDTH), gmla_s, gsb_s, mmla_s, msb_s,
                       womla, wosb, wout, tms)

    heads = lambda a, b, t: a.reshape(b, t, HEADS, SB_DIM)
    return (y_prompt.reshape(batch, seq, D_MODEL), y_sample.reshape(n_seq, dec, D_MODEL),
            mlakv_p.reshape(batch, seq, MLA_ROW), heads(sbk_p, batch, seq), heads(sbv_p, batch, seq),
            mlakv_s.reshape(n_seq, dec, MLA_ROW), heads(sbk_s, n_seq, dec), heads(sbv_s, n_seq, dec))
```

```python
import functools

import numpy as np
import jax
import jax.numpy as jnp
from jax import lax
from jax.experimental import pallas as pl
from jax.experimental.pallas import tpu as pltpu

F32 = jnp.float32
BF16 = jnp.bfloat16

D_MODEL = 1024
HEADS = 8
QK_NOPE = 64
QK_ROPE = 32
HALF_ROPE = QK_ROPE // 2
Q_HEAD = QK_NOPE + QK_ROPE
V_HEAD = 64
KV_RANK = 256
Q_RANK = 768
SB_DIM = 64
WIDTH = HEADS * V_HEAD
MLA_ROW = KV_RANK + QK_ROPE
PAGE = 128
ROPE_BASE = 10000.0
RMS_EPS = 1e-6
MLA_SCALE = Q_HEAD ** -0.5
SB_SCALE = SB_DIM ** -0.5

LANES = 128
QPAD = 3 * LANES
LOG2E = 1.4426950408889634
Q_SCALE = MLA_SCALE * LOG2E
SBQ_SCALE = SB_SCALE * LOG2E
SB_DEAD = -151.0
VMEM_LIMIT = 56 * 1024 * 1024

C_QA = 0
C_CKV = C_QA + Q_RANK
C_SBQ = C_CKV + KV_RANK
C_SBK = C_SBQ + WIDTH
C_SBV = C_SBK + WIDTH
C_GMLA = C_SBV + WIDTH
C_GSB = C_GMLA + WIDTH
C_MMLA = C_GSB + WIDTH
C_MSB = C_MMLA + D_MODEL
C_KR = C_MSB + D_MODEL
C_END = C_KR + LANES


def _rms(x, n):
    return x * lax.rsqrt(jnp.sum(x * x, axis=-1, keepdims=True) * (1.0 / n) + RMS_EPS)


def _sigmoid(x):
    return 1.0 / (1.0 + jnp.exp(-x))


def _split_bf16(x):
    hi = x.astype(BF16)
    lo = (x - hi.astype(F32)).astype(BF16)
    return hi, lo


def _dot(a, b):
    return jnp.dot(a, b, preferred_element_type=F32)


def _dot_nt(a, b):
    return lax.dot_general(a, b, (((1,), (1,)), ((), ())), preferred_element_type=F32)


def _proj_kernel(x_ref, cq_ref, sq_ref, ck_ref, sk_ref, normw_ref, w1_ref, qanw_ref,
                 wuq_ref, gq_ref, hsel_ref, wuk_ref, perm_ref, kvnw_ref, krnw_ref,
                 mlakv_ref, sbk_ref, sbv_ref, qmla_ref, kvbf_ref, sbq_ref, sbkbf_ref,
                 sbvbf_ref, gmla_ref, gsb_ref, mmla_ref, msb_ref):
    x = x_ref[...]
    xn = (_rms(x, D_MODEL) * normw_ref[...]).astype(BF16)

    def proj(a, b):
        return _dot(xn, w1_ref[:, a:b])

    q_a = proj(C_QA, C_CKV)
    qan = (_rms(q_a, Q_RANK) * qanw_ref[...]).astype(BF16)
    q = _dot(qan, wuq_ref[...])
    hi, lo = _split_bf16(q * q)
    ssq = _dot(hi, hsel_ref[...]) + _dot(lo, hsel_ref[...])
    qn = q * lax.rsqrt(ssq * (1.0 / Q_HEAD) + RMS_EPS) * gq_ref[...]
    x1 = qn[:, HEADS * QK_NOPE:HEADS * QK_NOPE + LANES]
    x2 = qn[:, HEADS * QK_NOPE + LANES:]
    cq = cq_ref[...]
    sq = sq_ref[...]
    rot = (jnp.concatenate([x1 * cq - x2 * sq, x2 * cq + x1 * sq], axis=1) * Q_SCALE).astype(BF16)
    rope = _dot(rot, perm_ref[...]).astype(BF16)
    for h in range(HEADS):
        pair = qn[:, LANES * (h // 2):LANES * (h // 2 + 1)].astype(BF16)
        q_lat = _dot(pair, wuk_ref[h]) * Q_SCALE
        qmla_ref[:, QPAD * h:QPAD * h + KV_RANK] = q_lat.astype(BF16)
        qmla_ref[:, QPAD * h + KV_RANK:QPAD * (h + 1)] = rope[:, LANES * h:LANES * (h + 1)]

    cn = _rms(proj(C_CKV, C_SBQ), KV_RANK) * kvnw_ref[...]
    k_r = proj(C_KR, C_END)
    krn = _rms(k_r, QK_ROPE) * krnw_ref[...]
    lane = lax.broadcasted_iota(jnp.int32, krn.shape, 1)
    swapped = jnp.where(lane < HALF_ROPE, pltpu.roll(krn, LANES - HALF_ROPE, 1), pltpu.roll(krn, HALF_ROPE, 1))
    kro = krn * ck_ref[...] + swapped * sk_ref[...]
    mlakv_ref[:, :KV_RANK] = cn
    mlakv_ref[:, KV_RANK:] = kro[:, :QK_ROPE]
    kvbf_ref[:, :KV_RANK] = cn.astype(BF16)
    kvbf_ref[:, KV_RANK:] = kro.astype(BF16)

    sbq_ref[...] = (proj(C_SBQ, C_SBK) * SBQ_SCALE).astype(BF16)
    sbk = proj(C_SBK, C_SBV)
    sbk_ref[...] = sbk
    sbkbf_ref[...] = sbk.astype(BF16)
    sbv = proj(C_SBV, C_GMLA)
    sbv_ref[...] = sbv
    sbvbf_ref[...] = sbv.astype(BF16)

    g = proj(C_GMLA, C_GSB)
    gmla_ref[...] = g * _sigmoid(g)
    g = proj(C_GSB, C_MMLA)
    gsb_ref[...] = g * _sigmoid(g)
    mmla_ref[...] = _sigmoid(proj(C_MMLA, C_MSB))
    msb_ref[...] = _sigmoid(proj(C_MSB, C_KR))


def _const_spec(shape):
    nd = len(shape)
    return pl.BlockSpec(shape, lambda *_: (0,) * nd, pipeline_mode=pl.Buffered(1))


def _proj(x, tables, table_map, weights, tm):
    n = x.shape[0]
    row = lambda w: pl.BlockSpec((tm, w), lambda i: (i, 0))
    tab = pl.BlockSpec((tm, LANES), table_map)
    widths = [(MLA_ROW, F32), (WIDTH, F32), (WIDTH, F32), (HEADS * QPAD, BF16), (QPAD, BF16),
              (WIDTH, BF16), (WIDTH, BF16), (WIDTH, BF16), (WIDTH, F32), (WIDTH, F32),
              (D_MODEL, F32), (D_MODEL, F32)]
    return pl.pallas_call(
        _proj_kernel,
        grid=(n // tm,),
        in_specs=[row(D_MODEL), tab, tab, tab, tab] + [_const_spec(w.shape) for w in weights],
        out_specs=[row(w) for w, _ in widths],
        out_shape=[jax.ShapeDtypeStruct((n, w), dt) for w, dt in widths],
        compiler_params=pltpu.CompilerParams(dimension_semantics=("arbitrary",), vmem_limit_bytes=VMEM_LIMIT),
        name="proj",
    )(x, *tables, *weights)


def _mla_prompt_kernel(q_ref, kv_ref, wuv_ref, o_ref, s_scr, m_scr, l_scr, acc_scr, *, tq):
    i = pl.program_id(1)
    tk = 2 * tq
    q = q_ref[...]
    qs = jnp.concatenate([q[:, QPAD * h:QPAD * (h + 1)] for h in range(HEADS)], axis=0)
    m_scr[...] = jnp.full(m_scr.shape, -jnp.inf, F32)
    l_scr[...] = jnp.zeros(l_scr.shape, F32)
    acc_scr[...] = jnp.zeros(acc_scr.shape, F32)
    last = i >> 1

    def keys(j):
        return kv_ref[pl.ds(pl.multiple_of(j * tk, tk), tk), :]

    def scores(buf, j):
        s_scr[buf] = _dot_nt(qs, keys(j))

    def update(buf, j, diagonal):
        s = s_scr[buf]
        if diagonal:
            t = lax.broadcasted_iota(jnp.int32, s.shape, 0) & (tq - 1)
            col = lax.broadcasted_iota(jnp.int32, s.shape, 1)
            s = jnp.where(col <= t + (i & 1) * tq, s, -jnp.inf)
        m_prev = m_scr[...]
        m_new = jnp.maximum(m_prev, jnp.max(s, axis=-1, keepdims=True))
        alpha = jnp.exp2(m_prev - m_new)
        p = jnp.exp2(s - jnp.tile(m_new, (1, tk // LANES)))
        l_scr[...] = alpha * l_scr[...] + jnp.sum(p, axis=-1, keepdims=True)
        acc_scr[...] = (jnp.tile(alpha, (1, KV_RANK // LANES)) * acc_scr[...]
                        + _dot(p.astype(BF16), keys(j)[:, :KV_RANK]))
        m_scr[...] = m_new

    scores(0, 0)

    def body(jj, carry):
        j = 2 * jj
        scores(1, j + 1)
        update(0, j, False)
        scores(0, j + 2)
        update(1, j + 1, False)
        return carry

    lax.fori_loop(0, last >> 1, body, 0)

    @pl.when((last & 1) == 0)
    def _():
        update(0, last, True)

    @pl.when((last & 1) == 1)
    def _():
        scores(1, last)
        update(0, last - 1, False)
        update(1, last, True)

    o_lat = (acc_scr[...] * jnp.tile(1.0 / l_scr[...], (1, KV_RANK // LANES))).astype(BF16)
    out = _dot(o_lat[:tq], wuv_ref[0])
    for h in range(1, HEADS):
        out += _dot(o_lat[h * tq:(h + 1) * tq], wuv_ref[h])
    o_ref[...] = out


def _mla_prompt(qmla, kvbf, wuv, batch, seq, tq):
    nq = seq // tq
    rows = HEADS * tq
    return pl.pallas_call(
        functools.partial(_mla_prompt_kernel, tq=tq),
        grid=(batch, nq),
        in_specs=[pl.BlockSpec((tq, HEADS * QPAD), lambda b, i: (b * nq + i, 0)),
                  pl.BlockSpec((seq, QPAD), lambda b, i: (b, 0)),
                  _const_spec(wuv.shape)],
        out_specs=pl.BlockSpec((tq, WIDTH), lambda b, i: (b * nq + i, 0)),
        out_shape=jax.ShapeDtypeStruct((batch * seq, WIDTH), F32),
        scratch_shapes=[pltpu.VMEM((2, rows, 2 * tq), F32), pltpu.VMEM((rows, LANES), F32),
                        pltpu.VMEM((rows, LANES), F32), pltpu.VMEM((rows, KV_RANK), F32)],
        compiler_params=pltpu.CompilerParams(dimension_semantics=("arbitrary", "arbitrary"),
                                             vmem_limit_bytes=VMEM_LIMIT),
        name="mla_prompt",
    )(qmla, kvbf, wuv)


def _tri2(tk):
    tri = np.tril(np.ones((tk, tk), np.float32))
    return jnp.asarray(np.concatenate([tri, tri], axis=0), BF16)


def _sb_weights(z, carry, tri2, mask):
    nz = -z
    lk = jnp.minimum(nz, 0.0) - jnp.log2(1.0 + jnp.exp2(jnp.minimum(z, nz)))
    if mask is not None:
        lk = jnp.where(mask, lk, 0.0)
    hi, lo = _split_bf16(lk)
    arg = z + _dot(jnp.concatenate([hi, lo], axis=1), tri2)
    total = jnp.sum(lk, axis=-1, keepdims=True)
    if carry is None:
        total = jnp.broadcast_to(total, (z.shape[0], LANES))
    else:
        arg = arg + jnp.tile(carry, (1, z.shape[1] // LANES))
        total = carry + total
    a = jnp.exp2(arg)
    if mask is not None:
        a = jnp.where(mask, a, 0.0)
    return a.astype(BF16), total


def _sb_prompt_kernel(q_ref, k_ref, v_ref, tri_ref, o_ref, carry_scr, acc_scr, *, tq):
    i = pl.program_id(1)
    q = q_ref[...]
    lane = lax.broadcasted_iota(jnp.int32, (tq, LANES), 1)
    zero = jnp.zeros((tq, LANES), BF16)
    groups = HEADS // 2
    qg = []
    for g in range(groups):
        pair = q[:, LANES * g:LANES * (g + 1)]
        qg.append(jnp.concatenate([jnp.where(lane < SB_DIM, pair, zero), jnp.where(lane >= SB_DIM, pair, zero)],
                                  axis=0))
    carry_scr[...] = jnp.zeros(carry_scr.shape, F32)
    acc_scr[...] = jnp.zeros(acc_scr.shape, F32)
    tri = tri_ref[...]

    def block(j, mask):
        start = pl.multiple_of(j * tq, tq)
        z = jnp.concatenate([_dot_nt(qg[g], k_ref[pl.ds(start, tq), LANES * g:LANES * (g + 1)])
                             for g in range(groups)], axis=0)
        a, carry = _sb_weights(z, carry_scr[...], tri, mask)
        carry_scr[...] = carry
        for g in range(groups):
            acc_scr[g] += _dot(a[2 * tq * g:2 * tq * (g + 1)], v_ref[pl.ds(start, tq), LANES * g:LANES * (g + 1)])

    t = lax.broadcasted_iota(jnp.int32, (HEADS * tq, tq), 0) & (tq - 1)
    col = lax.broadcasted_iota(jnp.int32, (HEADS * tq, tq), 1)
    block(i, col < t)

    def alive():
        return (jnp.max(carry_scr[...]) >= SB_DEAD).astype(jnp.int32)

    def cond(state):
        j, live = state
        return jnp.logical_and(j >= 0, live == 1)

    def body(state):
        j, _ = state
        block(j, None)
        return j - 1, alive()

    lax.while_loop(cond, body, (i - 1, alive()))

    for g in range(groups):
        o_ref[:, LANES * g:LANES * (g + 1)] = jnp.where(lane < SB_DIM, acc_scr[g, :tq], acc_scr[g, tq:])


def _sb_prompt(sbq, sbk, sbv, tri, batch, seq, tq):
    nq = seq // tq
    kv_spec = pl.BlockSpec((seq, WIDTH), lambda b, i: (b, 0), pipeline_mode=pl.Buffered(1))
    return pl.pallas_call(
        functools.partial(_sb_prompt_kernel, tq=tq),
        grid=(batch, nq),
        in_specs=[pl.BlockSpec((tq, WIDTH), lambda b, i: (b * nq + i, 0)), kv_spec, kv_spec,
                  _const_spec(tri.shape)],
        out_specs=pl.BlockSpec((tq, WIDTH), lambda b, i: (b * nq + i, 0)),
        out_shape=jax.ShapeDtypeStruct((batch * seq, WIDTH), F32),
        scratch_shapes=[pltpu.VMEM((HEADS * tq, LANES), F32), pltpu.VMEM((HEADS // 2, 2 * tq, LANES), F32)],
        compiler_params=pltpu.CompilerParams(dimension_semantics=("arbitrary", "arbitrary"),
                                             vmem_limit_bytes=VMEM_LIMIT),
        name="sb_prompt",
    )(sbq, sbk, sbv, tri)


SB_PREFETCH = 2
SB_AHEAD = SB_PREFETCH * PAGE


def _decode_kernel(pt_ref, q_ref, kvnew_ref, sbq_ref, sbknew_ref, sbvnew_ref, tri_ref, tripage_ref,
                   cmla_hbm, csbk_hbm, csbv_hbm, olat_ref, osb_ref,
                   kbuf, sbkbuf, sbvbuf, slowk, slowv, carry_scr, acc_scr,
                   sem_mla, sem_sbk, sem_sbv, sem_slow, *, n_seq, n_pages, dec):
    n = pl.program_id(0)
    slot = n & 1
    rows = HEADS * dec

    def mla_copy(seq, s, p):
        return pltpu.make_async_copy(cmla_hbm.at[pt_ref[seq, p]], kbuf.at[s, :, pl.ds(p * PAGE, PAGE)], sem_mla.at[s])

    def sb_copy(seq, s, r, src, dst, sem):
        return pltpu.make_async_copy(src.at[pt_ref[seq, n_pages - 1 - r]],
                                     dst.at[s, :, pl.ds((SB_PREFETCH - 1 - r) * PAGE, PAGE)], sem.at[s])

    def fetch(seq, s):
        for p in range(n_pages):
            mla_copy(seq, s, p).start()
        for r in range(SB_PREFETCH):
            sb_copy(seq, s, r, csbk_hbm, sbkbuf, sem_sbk).start()
            sb_copy(seq, s, r, csbv_hbm, sbvbuf, sem_sbv).start()

    @pl.when(n == 0)
    def _():
        fetch(0, 0)

    @pl.when(n + 1 < n_seq)
    def _():
        fetch(n + 1, 1 - slot)

    lane = lax.broadcasted_iota(jnp.int32, (rows, WIDTH), 1)
    rowi = lax.broadcasted_iota(jnp.int32, (rows, WIDTH), 0)
    own = (lane >> 6) == (rowi >> 3)
    qbd = jnp.where(own, jnp.tile(sbq_ref[0].astype(F32), (HEADS, 1)), 0.0).astype(BF16)
    tri = tri_ref[...]
    pad = jnp.zeros((PAGE - dec, WIDTH), F32)
    knew = jnp.concatenate([sbknew_ref[0], pad], axis=0).astype(BF16)
    vnew = jnp.concatenate([sbvnew_ref[0], pad], axis=0).astype(BF16)

    for r in range(SB_PREFETCH):
        sb_copy(n, slot, r, csbk_hbm, sbkbuf, sem_sbk).wait()
        sb_copy(n, slot, r, csbv_hbm, sbvbuf, sem_sbv).wait()

    for p in range(n_pages):
        pltpu.make_async_copy(cmla_hbm.at[0], kbuf.at[slot, :, pl.ds(p * PAGE, PAGE)], sem_mla.at[slot]).wait()

    qs = q_ref[0][:, :MLA_ROW]
    kc = kbuf[slot].astype(BF16)
    s = _dot(qs, kc)
    kn = jnp.concatenate([kvnew_ref[0], jnp.zeros((PAGE - dec, MLA_ROW), F32)], axis=0).astype(BF16)
    t_mla = lax.broadcasted_iota(jnp.int32, (rows, PAGE), 0) >> 3
    col = lax.broadcasted_iota(jnp.int32, (rows, PAGE), 1)
    s_new = jnp.where(col <= t_mla, _dot_nt(qs, kn), -jnp.inf)
    m = jnp.maximum(jnp.max(s, axis=-1, keepdims=True), jnp.max(s_new, axis=-1, keepdims=True))
    p = jnp.exp2(s - m)
    p_new = jnp.exp2(s_new - m)
    l = jnp.sum(p, axis=-1, keepdims=True) + jnp.sum(p_new, axis=-1, keepdims=True)
    o = _dot_nt(p.astype(BF16), kc[:KV_RANK]) + _dot(p_new.astype(BF16), kn[:, :KV_RANK])
    olat_ref[0] = o * (1.0 / l)

    kt = sbkbuf[slot].astype(BF16)
    vt = sbvbuf[slot].astype(BF16)
    z = jnp.concatenate([_dot(qbd, kt), _dot_nt(qbd, knew)], axis=1)
    t_sb = lax.broadcasted_iota(jnp.int32, z.shape, 0) & (dec - 1)
    col_sb = lax.broadcasted_iota(jnp.int32, z.shape, 1)
    a, carry = _sb_weights(z, None, tri, col_sb < SB_AHEAD + t_sb)
    acc_scr[...] = _dot_nt(a[:, :SB_AHEAD], vt) + _dot(a[:, SB_AHEAD:], vnew)
    carry_scr[...] = carry

    def alive():
        return (jnp.max(carry_scr[...]) >= SB_DEAD).astype(jnp.int32)

    def cond(state):
        p, live = state
        return jnp.logical_and(p >= 0, live == 1)

    def body(state):
        p, _ = state
        ck = pltpu.make_async_copy(csbk_hbm.at[pt_ref[n, p]], slowk, sem_slow.at[0])
        cv = pltpu.make_async_copy(csbv_hbm.at[pt_ref[n, p]], slowv, sem_slow.at[1])
        ck.start()
        cv.start()
        ck.wait()
        cv.wait()
        a, c = _sb_weights(_dot(qbd, slowk[...].astype(BF16)), carry_scr[...], tripage_ref[...], None)
        acc_scr[...] += _dot_nt(a, slowv[...].astype(BF16))
        carry_scr[...] = c
        return p - 1, alive()

    lax.while_loop(cond, body, (jnp.int32(n_pages - 1 - SB_PREFETCH), alive()))

    sel = jnp.where(own, acc_scr[...], 0.0)
    osb = sel[:dec]
    for h in range(1, HEADS):
        osb += sel[h * dec:(h + 1) * dec]
    osb_ref[0] = osb


def _decode(page_table, qmla, kvnew, sbq, sbknew, sbvnew, tri, tripage, cache_mla_t, cache_sbk_t, cache_sbv_t):
    n_seq, n_pages = page_table.shape
    dec = kvnew.shape[1]
    rows = HEADS * dec
    per_seq = lambda r, w: pl.BlockSpec((1, r, w), lambda n, pt: (n, 0, 0))
    const = lambda shape: pl.BlockSpec(shape, lambda n, pt: (0,) * len(shape), pipeline_mode=pl.Buffered(1))
    hbm = pl.BlockSpec(memory_space=pl.ANY)
    grid_spec = pltpu.PrefetchScalarGridSpec(
        num_scalar_prefetch=1,
        grid=(n_seq,),
        in_specs=[per_seq(rows, QPAD), per_seq(dec, MLA_ROW), per_seq(dec, WIDTH), per_seq(dec, WIDTH),
                  per_seq(dec, WIDTH), const(tri.shape), const(tripage.shape), hbm, hbm, hbm],
        out_specs=[per_seq(rows, KV_RANK), per_seq(dec, WIDTH)],
        scratch_shapes=[
            pltpu.VMEM((2, MLA_ROW, n_pages * PAGE), F32),
            pltpu.VMEM((2, WIDTH, SB_AHEAD), F32),
            pltpu.VMEM((2, WIDTH, SB_AHEAD), F32),
            pltpu.VMEM((WIDTH, PAGE), F32),
            pltpu.VMEM((WIDTH, PAGE), F32),
            pltpu.VMEM((rows, LANES), F32),
            pltpu.VMEM((rows, WIDTH), F32),
            pltpu.SemaphoreType.DMA((2,)),
            pltpu.SemaphoreType.DMA((2,)),
            pltpu.SemaphoreType.DMA((2,)),
            pltpu.SemaphoreType.DMA((2,)),
        ],
    )
    return pl.pallas_call(
        functools.partial(_decode_kernel, n_seq=n_seq, n_pages=n_pages, dec=dec),
        grid_spec=grid_spec,
        out_shape=[jax.ShapeDtypeStruct((n_seq, rows, KV_RANK), F32), jax.ShapeDtypeStruct((n_seq, dec, WIDTH), F32)],
        compiler_params=pltpu.CompilerParams(dimension_semantics=("arbitrary",), vmem_limit_bytes=VMEM_LIMIT),
        name="decode",
    )(page_table, qmla, kvnew, sbq, sbknew, sbvnew, tri, tripage, cache_mla_t, cache_sbk_t, cache_sbv_t)


def _uv_kernel(olat_ref, wuv_ref, o_ref):
    o = olat_ref[...].astype(BF16)
    out = _dot(o[:, :KV_RANK], wuv_ref[0])
    for h in range(1, HEADS):
        out += _dot(o[:, KV_RANK * h:KV_RANK * (h + 1)], wuv_ref[h])
    o_ref[...] = out


def _uv(olat, wuv, tm):
    n = olat.shape[0]
    return pl.pallas_call(
        _uv_kernel,
        grid=(n // tm,),
        in_specs=[pl.BlockSpec((tm, HEADS * KV_RANK), lambda i: (i, 0)), _const_spec(wuv.shape)],
        out_specs=pl.BlockSpec((tm, WIDTH), lambda i: (i, 0)),
        out_shape=jax.ShapeDtypeStruct((n, WIDTH), F32),
        compiler_params=pltpu.CompilerParams(dimension_semantics=("arbitrary",), vmem_limit_bytes=VMEM_LIMIT),
        name="uv",
    )(olat, wuv)


def _finish_kernel(x_ref, omla_ref, osb_ref, gmla_ref, gsb_ref, mmla_ref, msb_ref,
                   womla_ref, wosb_ref, wout_ref, y_ref):
    a = (omla_ref[...] * gmla_ref[...]).astype(BF16)
    b = (osb_ref[...] * gsb_ref[...]).astype(BF16)
    merged = mmla_ref[...] * _dot(a, womla_ref[...]) + msb_ref[...] * _dot(b, wosb_ref[...])
    y_ref[...] = x_ref[...] + _dot(merged.astype(BF16), wout_ref[...])


def _finish(x, omla, osb, gmla, gsb, mmla, msb, womla, wosb, wout, tm):
    n = x.shape[0]
    row = lambda w: pl.BlockSpec((tm, w), lambda i: (i, 0))
    return pl.pallas_call(
        _finish_kernel,
        grid=(n // tm,),
        in_specs=[row(D_MODEL), row(WIDTH), row(WIDTH), row(WIDTH), row(WIDTH), row(D_MODEL), row(D_MODEL),
                  _const_spec(womla.shape), _const_spec(wosb.shape), _const_spec(wout.shape)],
        out_specs=row(D_MODEL),
        out_shape=jax.ShapeDtypeStruct((n, D_MODEL), F32),
        compiler_params=pltpu.CompilerParams(dimension_semantics=("arbitrary",), vmem_limit_bytes=VMEM_LIMIT),
        name="finish",
    )(x, omla, osb, gmla, gsb, mmla, msb, womla, wosb, wout)


def _rope_tables(pos):
    inv_freq = ROPE_BASE ** (-jnp.arange(HALF_ROPE, dtype=F32) / HALF_ROPE)
    ang = pos.astype(F32)[:, None] * inv_freq[None, :]
    cos, sin = jnp.cos(ang), jnp.sin(ang)
    zeros = jnp.zeros((pos.shape[0], LANES - QK_ROPE), F32)
    return (jnp.tile(cos, (1, HEADS)), jnp.tile(sin, (1, HEADS)),
            jnp.concatenate([cos, cos, zeros], axis=1), jnp.concatenate([-sin, sin, zeros], axis=1))


def _head_select():
    head = np.concatenate([np.arange(HEADS * QK_NOPE) // QK_NOPE,
                           np.arange(LANES) // HALF_ROPE, np.arange(LANES) // HALF_ROPE])
    return jnp.asarray(head[:, None] == head[None, :], BF16)


def _rope_perm():
    p = np.zeros((2 * LANES, HEADS * LANES), np.float32)
    for h in range(HEADS):
        for d in range(HALF_ROPE):
            p[h * HALF_ROPE + d, h * LANES + d] = 1.0
            p[LANES + h * HALF_ROPE + d, h * LANES + HALF_ROPE + d] = 1.0
    return jnp.asarray(p, BF16)


def _pack_weights(norm_w, w_in, mla_q_a_norm_w, mla_w_uq, mla_q_head_norm_w, mla_w_uk,
                  mla_kv_a_norm_w, mla_k_rope_norm_w):
    sizes = (Q_RANK, KV_RANK, QK_ROPE, WIDTH, WIDTH, WIDTH, WIDTH, WIDTH, D_MODEL, D_MODEL)
    parts, start = [], 0
    for size in sizes:
        parts.append(w_in[:, start:start + size])
        start += size
    q_a, c_kv, k_r, sb_q, sb_k, sb_v, g_mla, g_sb, m_mla, m_sb = parts
    w1 = jnp.concatenate([q_a, c_kv, sb_q, sb_k, sb_v, g_mla, g_sb, m_mla, m_sb, k_r,
                          jnp.zeros((D_MODEL, LANES - QK_ROPE), F32)], axis=1).astype(BF16)
    wuq = mla_w_uq.reshape(Q_RANK, HEADS, Q_HEAD)
    wuq = jnp.concatenate([wuq[:, :, :QK_NOPE].reshape(Q_RANK, -1),
                           wuq[:, :, QK_NOPE:QK_NOPE + HALF_ROPE].reshape(Q_RANK, -1),
                           wuq[:, :, QK_NOPE + HALF_ROPE:].reshape(Q_RANK, -1)], axis=1).astype(BF16)
    g = mla_q_head_norm_w
    gq = jnp.concatenate([jnp.tile(g[:QK_NOPE], HEADS), jnp.tile(g[QK_NOPE:QK_NOPE + HALF_ROPE], HEADS),
                          jnp.tile(g[QK_NOPE + HALF_ROPE:], HEADS)])[None, :]
    wuk_t = jnp.transpose(mla_w_uk, (1, 2, 0))
    zeros = jnp.zeros_like(wuk_t)
    odd = (jnp.arange(HEADS) % 2 == 1)[:, None, None]
    wuk = jnp.concatenate([jnp.where(odd, zeros, wuk_t), jnp.where(odd, wuk_t, zeros)], axis=1).astype(BF16)
    krnw = jnp.concatenate([mla_k_rope_norm_w, jnp.zeros((LANES - QK_ROPE,), F32)])[None, :]
    return [norm_w[None, :], w1, mla_q_a_norm_w[None, :], wuq, gq, _head_select(), wuk, _rope_perm(),
            mla_kv_a_norm_w[None, :], krnw]


def _pack_wuv(mla_w_uv):
    tiled = jnp.tile(jnp.transpose(mla_w_uv, (1, 0, 2)), (1, 1, HEADS))
    own = (jnp.arange(WIDTH) // V_HEAD)[None, None, :] == jnp.arange(HEADS)[:, None, None]
    return jnp.where(own, tiled, 0.0).astype(BF16)


def _tile(n, cap):
    t = cap
    while n % t:
        t //= 2
    return t


def kernel(x_prompt, x_sample, cache_mla_kv, cache_sb_k, cache_sb_v, page_table, norm_w, w_in,
           mla_q_a_norm_w, mla_w_uq, mla_q_head_norm_w, mla_w_uk, mla_kv_a_norm_w, mla_k_rope_norm_w,
           mla_w_uv, w_o_mla, w_o_sb, w_out):
    batch, seq, _ = x_prompt.shape
    n_seq, dec, _ = x_sample.shape
    n_pages = page_table.shape[1]
    past = n_pages * PAGE
    assert n_pages > SB_PREFETCH and dec == 8 and HEADS == 8 and seq % 512 == 0

    weights = _pack_weights(norm_w, w_in, mla_q_a_norm_w, mla_w_uq, mla_q_head_norm_w, mla_w_uk,
                            mla_kv_a_norm_w, mla_k_rope_norm_w)
    wuv = _pack_wuv(mla_w_uv)
    womla, wosb, wout = w_o_mla.astype(BF16), w_o_sb.astype(BF16), w_out.astype(BF16)

    tq = _tile(seq, 256)
    tm = _tile(seq, 256)
    xp = x_prompt.reshape(batch * seq, D_MODEL)
    tabs = _rope_tables(jnp.arange(seq, dtype=jnp.int32))
    ntile = seq // tm
    (mlakv_p, sbk_p, sbv_p, qmla, kvbf, sbq, sbkbf, sbvbf, gmla, gsb, mmla, msb) = _proj(
        xp, tabs, lambda i: (i % ntile, 0), weights, tm)
    tri = _tri2(tq)
    omla = _mla_prompt(qmla, kvbf, wuv, batch, seq, tq)
    osb = _sb_prompt(sbq, sbkbf, sbvbf, tri, batch, seq, tq)
    y_prompt = _finish(xp, omla, osb, gmla, gsb, mmla, msb, womla, wosb, wout, tm)

    ns = n_seq * dec
    tms = _tile(ns, 256)
    xs = x_sample.reshape(ns, D_MODEL)
    tabs_s = [jnp.tile(t, (tms // dec, 1)) for t in _rope_tables(past + jnp.arange(dec, dtype=jnp.int32))]
    (mlakv_s, sbk_s, sbv_s, qmla_s, _, sbq_s, _, _, gmla_s, gsb_s, mmla_s, msb_s) = _proj(
        xs, tabs_s, lambda i: (0, 0), weights, tms)
    tri_s = _tri2(SB_AHEAD + PAGE)
    per_seq = lambda a: a.reshape(n_seq, dec, a.shape[-1])
    pool = cache_mla_kv.shape[0]
    cache_mla_t = jnp.transpose(cache_mla_kv, (0, 2, 1))
    cache_sbk_t = jnp.transpose(cache_sb_k, (0, 2, 3, 1)).reshape(pool, WIDTH, PAGE)
    cache_sbv_t = jnp.transpose(cache_sb_v, (0, 2, 3, 1)).reshape(pool, WIDTH, PAGE)
    olat_s, osb_s = _decode(page_table, qmla_s.reshape(n_seq, dec * HEADS, QPAD), per_seq(mlakv_s), per_seq(sbq_s),
                            per_seq(sbk_s), per_seq(sbv_s), tri_s, _tri2(PAGE), cache_mla_t, cache_sbk_t, cache_sbv_t)
    omla_s = _uv(olat_s.reshape(ns, HEADS * KV_RANK), wuv, tms)
    y_sample = _finish(xs, omla_s, osb_s.reshape(ns, WIDTH), gmla_s, gsb_s, mmla_s, msb_s,
                       womla, wosb, wout, tms)

    heads = lambda a, b, t: a.reshape(b, t, HEADS, SB_DIM)
    return (y_prompt.reshape(batch, seq, D_MODEL), y_sample.reshape(n_seq, dec, D_MODEL),
            mlakv_p.reshape(batch, seq, MLA_ROW), heads(sbk_p, batch, seq), heads(sbv_p, batch, seq),
            mlakv_s.reshape(n_seq, dec, MLA_ROW), heads(sbk_s, n_seq, dec), heads(sbv_s, n_seq, dec))
```

```python
import functools

import numpy as np
import jax
import jax.numpy as jnp
from jax import lax
from jax.experimental import pallas as pl
from jax.experimental.pallas import tpu as pltpu

F32 = jnp.float32
BF16 = jnp.bfloat16

D_MODEL = 1024
HEADS = 8
QK_NOPE = 64
QK_ROPE = 32
HALF_ROPE = QK_ROPE // 2
Q_HEAD = QK_NOPE + QK_ROPE
V_HEAD = 64
KV_RANK = 256
Q_RANK = 768
SB_DIM = 64
WIDTH = HEADS * V_HEAD
MLA_ROW = KV_RANK + QK_ROPE
PAGE = 128
ROPE_BASE = 10000.0
RMS_EPS = 1e-6
MLA_SCALE = Q_HEAD ** -0.5
SB_SCALE = SB_DIM ** -0.5

LANES = 128
QPAD = 3 * LANES
LOG2E = 1.4426950408889634
Q_SCALE = MLA_SCALE * LOG2E
SBQ_SCALE = SB_SCALE * LOG2E
SB_DEAD = -151.0
VMEM_LIMIT = 56 * 1024 * 1024

C_QA = 0
C_CKV = C_QA + Q_RANK
C_SBQ = C_CKV + KV_RANK
C_SBK = C_SBQ + WIDTH
C_SBV = C_SBK + WIDTH
C_GMLA = C_SBV + WIDTH
C_GSB = C_GMLA + WIDTH
C_MMLA = C_GSB + WIDTH
C_MSB = C_MMLA + D_MODEL
C_KR = C_MSB + D_MODEL
C_END = C_KR + LANES


def _rms(x, n):
    return x * lax.rsqrt(jnp.sum(x * x, axis=-1, keepdims=True) * (1.0 / n) + RMS_EPS)


def _sigmoid(x):
    return 1.0 / (1.0 + jnp.exp(-x))


def _split_bf16(x):
    hi = x.astype(BF16)
    lo = (x - hi.astype(F32)).astype(BF16)
    return hi, lo


def _dot(a, b):
    return jnp.dot(a, b, preferred_element_type=F32)


def _dot_nt(a, b):
    return lax.dot_general(a, b, (((1,), (1,)), ((), ())), preferred_element_type=F32)


def _proj_kernel(x_ref, cq_ref, sq_ref, ck_ref, sk_ref, normw_ref, w1_ref, qanw_ref,
                 wuq_ref, gq_ref, wuk_ref, perm_ref, kvnw_ref, krnw_ref,
                 mlakv_ref, sbk_ref, sbv_ref, qmla_ref, kvbf_ref, sbq_ref, sbkbf_ref,
                 sbvbf_ref, gmla_ref, gsb_ref, mmla_ref, msb_ref, *, transposed):
    x = x_ref[...]
    xn = (_rms(x, D_MODEL) * normw_ref[...]).astype(BF16)

    def proj(a, b):
        return _dot(xn, w1_ref[:, a:b])

    q_a = proj(C_QA, C_CKV)
    qan = (_rms(q_a, Q_RANK) * qanw_ref[...]).astype(BF16)
    q = _dot(qan, wuq_ref[...])
    q2 = q * q
    nope_w = HEADS * QK_NOPE
    rope2 = q2[:, nope_w:nope_w + LANES] + q2[:, nope_w + LANES:]
    lane = lax.broadcasted_iota(jnp.int32, rope2.shape, 1)
    upper = lane >= QK_NOPE
    rope_head = lane >> (HALF_ROPE.bit_length() - 1)
    inv = []
    for h in range(HEADS):
        grp = q2[:, LANES * (h // 2):LANES * (h // 2 + 1)]
        mine = jnp.where(upper if h % 2 else ~upper, grp, 0.0) + jnp.where(rope_head == h, rope2, 0.0)
        inv.append(lax.rsqrt(jnp.sum(mine, axis=-1, keepdims=True) * (1.0 / Q_HEAD) + RMS_EPS))
    rope_inv = inv[0]
    for h in range(1, HEADS):
        rope_inv = jnp.where(rope_head == h, inv[h], rope_inv)
    scale = jnp.concatenate([jnp.where(upper, inv[2 * g + 1], inv[2 * g]) for g in range(HEADS // 2)]
                            + [rope_inv, rope_inv], axis=1)
    qn = q * scale * gq_ref[...]
    x1 = qn[:, nope_w:nope_w + LANES]
    x2 = qn[:, nope_w + LANES:]
    cq = cq_ref[...]
    sq = sq_ref[...]
    rot = (jnp.concatenate([x1 * cq - x2 * sq, x2 * cq + x1 * sq], axis=1) * Q_SCALE).astype(BF16)
    rope = _dot(rot, perm_ref[...]).astype(BF16)
    for h in range(HEADS):
        pair = qn[:, LANES * (h // 2):LANES * (h // 2 + 1)].astype(BF16)
        q_lat = _dot(pair, wuk_ref[h]) * Q_SCALE
        qmla_ref[:, QPAD * h:QPAD * h + KV_RANK] = q_lat.astype(BF16)
        qmla_ref[:, QPAD * h + KV_RANK:QPAD * (h + 1)] = rope[:, LANES * h:LANES * (h + 1)]

    cn = _rms(proj(C_CKV, C_SBQ), KV_RANK) * kvnw_ref[...]
    k_r = proj(C_KR, C_END)
    krn = _rms(k_r, QK_ROPE) * krnw_ref[...]
    lane = lax.broadcasted_iota(jnp.int32, krn.shape, 1)
    swapped = jnp.where(lane < HALF_ROPE, pltpu.roll(krn, LANES - HALF_ROPE, 1), pltpu.roll(krn, HALF_ROPE, 1))
    kro = krn * ck_ref[...] + swapped * sk_ref[...]
    if transposed:
        mlakv_ref[0, :KV_RANK, :] = cn.T
        mlakv_ref[0, KV_RANK:, :] = kro.T[:QK_ROPE]
    else:
        mlakv_ref[:, :KV_RANK] = cn
        mlakv_ref[:, KV_RANK:] = kro[:, :QK_ROPE]
    kvbf_ref[:, :KV_RANK] = cn.astype(BF16)
    kvbf_ref[:, KV_RANK:] = kro.astype(BF16)

    sbq_ref[...] = (proj(C_SBQ, C_SBK) * SBQ_SCALE).astype(BF16)
    for c0, f32_ref, bf_ref in ((C_SBK, sbk_ref, sbkbf_ref), (C_SBV, sbv_ref, sbvbf_ref)):
        val = proj(c0, c0 + WIDTH)
        if transposed:
            f32_ref[0] = val.T
        else:
            f32_ref[...] = val
        bf_ref[...] = val.astype(BF16)

    g = proj(C_GMLA, C_GSB)
    gmla_ref[...] = (g * _sigmoid(g)).astype(BF16)
    g = proj(C_GSB, C_MMLA)
    gsb_ref[...] = (g * _sigmoid(g)).astype(BF16)
    mmla_ref[...] = _sigmoid(proj(C_MMLA, C_MSB)).astype(BF16)
    msb_ref[...] = _sigmoid(proj(C_MSB, C_KR)).astype(BF16)


def _const_spec(shape):
    nd = len(shape)
    return pl.BlockSpec(shape, lambda *_: (0,) * nd, pipeline_mode=pl.Buffered(1))


def _proj(x, tables, table_map, weights, tm, groups=None):
    n = x.shape[0]
    row = lambda w: pl.BlockSpec((tm, w), lambda i: (i, 0))
    tab = pl.BlockSpec((tm, LANES), table_map)
    widths = [(MLA_ROW, F32), (WIDTH, F32), (WIDTH, F32), (HEADS * QPAD, BF16), (QPAD, BF16),
              (WIDTH, BF16), (WIDTH, BF16), (WIDTH, BF16), (WIDTH, BF16), (WIDTH, BF16),
              (D_MODEL, BF16), (D_MODEL, BF16)]
    out_specs = [row(w) for w, _ in widths]
    out_shape = [jax.ShapeDtypeStruct((n, w), dt) for w, dt in widths]
    if groups is not None:
        per = n // groups
        ntile = per // tm
        for o in range(3):
            w = widths[o][0]
            out_specs[o] = pl.BlockSpec((1, w, tm), lambda i: (i // ntile, 0, i % ntile))
            out_shape[o] = jax.ShapeDtypeStruct((groups, w, per), F32)
    return pl.pallas_call(
        functools.partial(_proj_kernel, transposed=groups is not None),
        grid=(n // tm,),
        in_specs=[row(D_MODEL), tab, tab, tab, tab] + [_const_spec(w.shape) for w in weights],
        out_specs=out_specs,
        out_shape=out_shape,
        compiler_params=pltpu.CompilerParams(dimension_semantics=("arbitrary",), vmem_limit_bytes=VMEM_LIMIT),
        name="proj",
    )(x, *tables, *weights)


def _mla_prompt_kernel(q_ref, kv_ref, wuv_ref, o_ref, s_scr, m_scr, l_scr, acc_scr, *, tq):
    i = pl.program_id(1)
    tk = 2 * tq
    q = q_ref[...]
    qs = jnp.concatenate([q[:, QPAD * h:QPAD * (h + 1)] for h in range(HEADS)], axis=0)
    m_scr[...] = jnp.full(m_scr.shape, -jnp.inf, F32)
    l_scr[...] = jnp.zeros(l_scr.shape, F32)
    acc_scr[...] = jnp.zeros(acc_scr.shape, F32)
    last = i >> 1

    def keys(j):
        return kv_ref[pl.ds(pl.multiple_of(j * tk, tk), tk), :]

    def scores(buf, j):
        s_scr[buf] = _dot_nt(qs, keys(j))

    def update(buf, j, diagonal):
        s = s_scr[buf]
        if diagonal:
            t = lax.broadcasted_iota(jnp.int32, s.shape, 0) & (tq - 1)
            col = lax.broadcasted_iota(jnp.int32, s.shape, 1)
            s = jnp.where(col <= t + (i & 1) * tq, s, -jnp.inf)
        m_prev = m_scr[...]
        m_new = jnp.maximum(m_prev, jnp.max(s, axis=-1, keepdims=True))
        alpha = jnp.exp2(m_prev - m_new)
        p = jnp.exp2(s - jnp.tile(m_new, (1, tk // LANES)))
        l_scr[...] = alpha * l_scr[...] + jnp.sum(p, axis=-1, keepdims=True)
        acc_scr[...] = (jnp.tile(alpha, (1, KV_RANK // LANES)) * acc_scr[...]
                        + _dot(p.astype(BF16), keys(j)[:, :KV_RANK]))
        m_scr[...] = m_new

    scores(0, 0)

    def body(jj, carry):
        j = 2 * jj
        scores(1, j + 1)
        update(0, j, False)
        scores(0, j + 2)
        update(1, j + 1, False)
        return carry

    lax.fori_loop(0, last >> 1, body, 0)

    @pl.when((last & 1) == 0)
    def _():
        update(0, last, True)

    @pl.when((last & 1) == 1)
    def _():
        scores(1, last)
        update(0, last - 1, False)
        update(1, last, True)

    o_lat = (acc_scr[...] * jnp.tile(1.0 / l_scr[...], (1, KV_RANK // LANES))).astype(BF16)
    out = _dot(o_lat[:tq], wuv_ref[0])
    for h in range(1, HEADS):
        out += _dot(o_lat[h * tq:(h + 1) * tq], wuv_ref[h])
    o_ref[...] = out


def _mla_prompt(qmla, kvbf, wuv, batch, seq, tq):
    nq = seq // tq
    rows = HEADS * tq
    return pl.pallas_call(
        functools.partial(_mla_prompt_kernel, tq=tq),
        grid=(batch, nq),
        in_specs=[pl.BlockSpec((tq, HEADS * QPAD), lambda b, i: (b * nq + i, 0)),
                  pl.BlockSpec((seq, QPAD), lambda b, i: (b, 0)),
                  _const_spec(wuv.shape)],
        out_specs=pl.BlockSpec((tq, WIDTH), lambda b, i: (b * nq + i, 0)),
        out_shape=jax.ShapeDtypeStruct((batch * seq, WIDTH), F32),
        scratch_shapes=[pltpu.VMEM((2, rows, 2 * tq), F32), pltpu.VMEM((rows, LANES), F32),
                        pltpu.VMEM((rows, LANES), F32), pltpu.VMEM((rows, KV_RANK), F32)],
        compiler_params=pltpu.CompilerParams(dimension_semantics=("arbitrary", "arbitrary"),
                                             vmem_limit_bytes=VMEM_LIMIT),
        name="mla_prompt",
    )(qmla, kvbf, wuv)


def _tri2(tk):
    tri = np.tril(np.ones((tk, tk), np.float32))
    return jnp.asarray(np.concatenate([tri, tri], axis=0), BF16)


def _sb_weights(z, carry, tri2, mask):
    nz = -z
    lk = jnp.minimum(nz, 0.0) - jnp.log2(1.0 + jnp.exp2(jnp.minimum(z, nz)))
    if mask is not None:
        lk = jnp.where(mask, lk, 0.0)
    hi, lo = _split_bf16(lk)
    arg = z + _dot(jnp.concatenate([hi, lo], axis=1), tri2)
    total = jnp.sum(lk, axis=-1, keepdims=True)
    if carry is None:
        total = jnp.broadcast_to(total, (z.shape[0], LANES))
    else:
        arg = arg + jnp.tile(carry, (1, z.shape[1] // LANES))
        total = carry + total
    a = jnp.exp2(arg)
    if mask is not None:
        a = jnp.where(mask, a, 0.0)
    return a.astype(BF16), total


def _sb_prompt_kernel(q_ref, k_ref, v_ref, tri_ref, o_ref, carry_scr, acc_scr, *, tq):
    i = pl.program_id(1)
    q = q_ref[...]
    lane = lax.broadcasted_iota(jnp.int32, (tq, LANES), 1)
    zero = jnp.zeros((tq, LANES), BF16)
    groups = HEADS // 2
    qg = []
    for g in range(groups):
        pair = q[:, LANES * g:LANES * (g + 1)]
        qg.append(jnp.concatenate([jnp.where(lane < SB_DIM, pair, zero), jnp.where(lane >= SB_DIM, pair, zero)],
                                  axis=0))
    carry_scr[...] = jnp.zeros(carry_scr.shape, F32)
    acc_scr[...] = jnp.zeros(acc_scr.shape, F32)
    tri = tri_ref[...]

    def block(j, mask):
        start = pl.multiple_of(j * tq, tq)
        z = jnp.concatenate([_dot_nt(qg[g], k_ref[pl.ds(start, tq), LANES * g:LANES * (g + 1)])
                             for g in range(groups)], axis=0)
        a, carry = _sb_weights(z, carry_scr[...], tri, mask)
        carry_scr[...] = carry
        for g in range(groups):
            acc_scr[g] += _dot(a[2 * tq * g:2 * tq * (g + 1)], v_ref[pl.ds(start, tq), LANES * g:LANES * (g + 1)])

    t = lax.broadcasted_iota(jnp.int32, (HEADS * tq, tq), 0) & (tq - 1)
    col = lax.broadcasted_iota(jnp.int32, (HEADS * tq, tq), 1)
    block(i, col < t)

    def alive():
        return (jnp.max(carry_scr[...]) >= SB_DEAD).astype(jnp.int32)

    def cond(state):
        j, live = state
        return jnp.logical_and(j >= 0, live == 1)

    def body(state):
        j, _ = state
        block(j, None)
        return j - 1, alive()

    lax.while_loop(cond, body, (i - 1, alive()))

    for g in range(groups):
        o_ref[:, LANES * g:LANES * (g + 1)] = jnp.where(lane < SB_DIM, acc_scr[g, :tq], acc_scr[g, tq:])


def _sb_prompt(sbq, sbk, sbv, tri, batch, seq, tq):
    nq = seq // tq
    kv_spec = pl.BlockSpec((seq, WIDTH), lambda b, i: (b, 0), pipeline_mode=pl.Buffered(1))
    return pl.pallas_call(
        functools.partial(_sb_prompt_kernel, tq=tq),
        grid=(batch, nq),
        in_specs=[pl.BlockSpec((tq, WIDTH), lambda b, i: (b * nq + i, 0)), kv_spec, kv_spec,
                  _const_spec(tri.shape)],
        out_specs=pl.BlockSpec((tq, WIDTH), lambda b, i: (b * nq + i, 0)),
        out_shape=jax.ShapeDtypeStruct((batch * seq, WIDTH), F32),
        scratch_shapes=[pltpu.VMEM((HEADS * tq, LANES), F32), pltpu.VMEM((HEADS // 2, 2 * tq, LANES), F32)],
        compiler_params=pltpu.CompilerParams(dimension_semantics=("arbitrary", "arbitrary"),
                                             vmem_limit_bytes=VMEM_LIMIT),
        name="sb_prompt",
    )(sbq, sbk, sbv, tri)


SB_PREFETCH = 2
SB_AHEAD = SB_PREFETCH * PAGE


def _decode_kernel(pt_ref, q_ref, kvnew_ref, sbq_ref, sbknew_ref, sbvnew_ref, tri_ref, tripage_ref,
                   cmla_hbm, csbk_hbm, csbv_hbm, olat_ref, osb_ref,
                   kbuf, sbkbuf, sbvbuf, slowk, slowv, carry_scr, acc_scr,
                   sem_mla, sem_sbk, sem_sbv, sem_slow, *, n_seq, n_pages, dec):
    n = pl.program_id(0)
    slot = n & 1
    rows = HEADS * dec

    def mla_copy(seq, s, p):
        return pltpu.make_async_copy(cmla_hbm.at[pt_ref[seq, p]], kbuf.at[s, :, pl.ds(p * PAGE, PAGE)], sem_mla.at[s])

    def sb_copy(seq, s, r, src, dst, sem):
        return pltpu.make_async_copy(src.at[pt_ref[seq, n_pages - 1 - r]],
                                     dst.at[s, :, pl.ds((SB_PREFETCH - 1 - r) * PAGE, PAGE)], sem.at[s])

    def fetch(seq, s):
        for p in range(n_pages):
            mla_copy(seq, s, p).start()
        for r in range(SB_PREFETCH):
            sb_copy(seq, s, r, csbk_hbm, sbkbuf, sem_sbk).start()
            sb_copy(seq, s, r, csbv_hbm, sbvbuf, sem_sbv).start()

    @pl.when(n == 0)
    def _():
        fetch(0, 0)

    @pl.when(n + 1 < n_seq)
    def _():
        fetch(n + 1, 1 - slot)

    lane = lax.broadcasted_iota(jnp.int32, (rows, WIDTH), 1)
    rowi = lax.broadcasted_iota(jnp.int32, (rows, WIDTH), 0)
    own = (lane >> 6) == (rowi >> 3)
    qbd = jnp.where(own, jnp.tile(sbq_ref[0].astype(F32), (HEADS, 1)), 0.0).astype(BF16)
    tri = tri_ref[...]
    pad = jnp.zeros((PAGE - dec, WIDTH), F32)
    knew = jnp.concatenate([sbknew_ref[0], pad], axis=0).astype(BF16)
    vnew = jnp.concatenate([sbvnew_ref[0], pad], axis=0).astype(BF16)

    for r in range(SB_PREFETCH):
        sb_copy(n, slot, r, csbk_hbm, sbkbuf, sem_sbk).wait()
        sb_copy(n, slot, r, csbv_hbm, sbvbuf, sem_sbv).wait()

    for p in range(n_pages):
        pltpu.make_async_copy(cmla_hbm.at[0], kbuf.at[slot, :, pl.ds(p * PAGE, PAGE)], sem_mla.at[slot]).wait()

    qf = q_ref[0].astype(F32)
    qs = jnp.concatenate([qf[:, QPAD * h:QPAD * h + MLA_ROW] for h in range(HEADS)], axis=0).astype(BF16)
    kc = kbuf[slot].astype(BF16)
    s = _dot(qs, kc)
    kn = jnp.concatenate([kvnew_ref[0], jnp.zeros((PAGE - dec, MLA_ROW), F32)], axis=0).astype(BF16)
    t_mla = lax.broadcasted_iota(jnp.int32, (rows, PAGE), 0) & (dec - 1)
    col = lax.broadcasted_iota(jnp.int32, (rows, PAGE), 1)
    s_new = jnp.where(col <= t_mla, _dot_nt(qs, kn), -jnp.inf)
    m = jnp.maximum(jnp.max(s, axis=-1, keepdims=True), jnp.max(s_new, axis=-1, keepdims=True))
    p = jnp.exp2(s - m)
    p_new = jnp.exp2(s_new - m)
    l = jnp.sum(p, axis=-1, keepdims=True) + jnp.sum(p_new, axis=-1, keepdims=True)
    o = _dot_nt(p.astype(BF16), kc[:KV_RANK]) + _dot(p_new.astype(BF16), kn[:, :KV_RANK])
    olat_ref[0] = o * (1.0 / l)

    kt = sbkbuf[slot].astype(BF16)
    vt = sbvbuf[slot].astype(BF16)
    z = jnp.concatenate([_dot(qbd, kt), _dot_nt(qbd, knew)], axis=1)
    t_sb = lax.broadcasted_iota(jnp.int32, z.shape, 0) & (dec - 1)
    col_sb = lax.broadcasted_iota(jnp.int32, z.shape, 1)
    a, carry0 = _sb_weights(z, None, tri, col_sb < SB_AHEAD + t_sb)
    acc0 = _dot_nt(a[:, :SB_AHEAD], vt) + _dot(a[:, SB_AHEAD:], vnew)

    def own_heads(acc):
        sel = jnp.where(own, acc, 0.0)
        out = sel[:dec]
        for h in range(1, HEADS):
            out += sel[h * dec:(h + 1) * dec]
        return out

    osb_ref[0] = own_heads(acc0)

    @pl.when(jnp.max(carry0) >= SB_DEAD)
    def _():
        acc_scr[...] = acc0
        carry_scr[...] = carry0

        def cond(state):
            p, live = state
            return jnp.logical_and(p >= 0, live == 1)

        def body(state):
            p, _ = state
            ck = pltpu.make_async_copy(csbk_hbm.at[pt_ref[n, p]], slowk, sem_slow.at[0])
            cv = pltpu.make_async_copy(csbv_hbm.at[pt_ref[n, p]], slowv, sem_slow.at[1])
            ck.start()
            cv.start()
            ck.wait()
            cv.wait()
            a, c = _sb_weights(_dot(qbd, slowk[...].astype(BF16)), carry_scr[...], tripage_ref[...], None)
            acc_scr[...] += _dot_nt(a, slowv[...].astype(BF16))
            carry_scr[...] = c
            return p - 1, (jnp.max(c) >= SB_DEAD).astype(jnp.int32)

        lax.while_loop(cond, body, (jnp.int32(n_pages - 1 - SB_PREFETCH), jnp.int32(1)))
        osb_ref[0] = own_heads(acc_scr[...])


def _decode(page_table, qmla, kvnew, sbq, sbknew, sbvnew, tri, tripage, cache_mla_t, cache_sbk_t, cache_sbv_t):
    n_seq, n_pages = page_table.shape
    dec = kvnew.shape[1]
    rows = HEADS * dec
    per_seq = lambda r, w: pl.BlockSpec((1, r, w), lambda n, pt: (n, 0, 0))
    const = lambda shape: pl.BlockSpec(shape, lambda n, pt: (0,) * len(shape), pipeline_mode=pl.Buffered(1))
    hbm = pl.BlockSpec(memory_space=pl.ANY)
    grid_spec = pltpu.PrefetchScalarGridSpec(
        num_scalar_prefetch=1,
        grid=(n_seq,),
        in_specs=[per_seq(dec, HEADS * QPAD), per_seq(dec, MLA_ROW), per_seq(dec, WIDTH), per_seq(dec, WIDTH),
                  per_seq(dec, WIDTH), const(tri.shape), const(tripage.shape), hbm, hbm, hbm],
        out_specs=[per_seq(rows, KV_RANK), per_seq(dec, WIDTH)],
        scratch_shapes=[
            pltpu.VMEM((2, MLA_ROW, n_pages * PAGE), F32),
            pltpu.VMEM((2, WIDTH, SB_AHEAD), F32),
            pltpu.VMEM((2, WIDTH, SB_AHEAD), F32),
            pltpu.VMEM((WIDTH, PAGE), F32),
            pltpu.VMEM((WIDTH, PAGE), F32),
            pltpu.VMEM((rows, LANES), F32),
            pltpu.VMEM((rows, WIDTH), F32),
            pltpu.SemaphoreType.DMA((2,)),
            pltpu.SemaphoreType.DMA((2,)),
            pltpu.SemaphoreType.DMA((2,)),
            pltpu.SemaphoreType.DMA((2,)),
        ],
    )
    return pl.pallas_call(
        functools.partial(_decode_kernel, n_seq=n_seq, n_pages=n_pages, dec=dec),
        grid_spec=grid_spec,
        out_shape=[jax.ShapeDtypeStruct((n_seq, rows, KV_RANK), F32), jax.ShapeDtypeStruct((n_seq, dec, WIDTH), F32)],
        compiler_params=pltpu.CompilerParams(dimension_semantics=("arbitrary",), vmem_limit_bytes=VMEM_LIMIT),
        name="decode",
    )(page_table, qmla, kvnew, sbq, sbknew, sbvnew, tri, tripage, cache_mla_t, cache_sbk_t, cache_sbv_t)


def _uv_kernel(olat_ref, wuv_ref, o_ref, *, dec):
    o = olat_ref[...]
    seqs = o.shape[0]
    out = None
    for h in range(HEADS):
        oh = o[:, h * dec:(h + 1) * dec, :].reshape(seqs * dec, KV_RANK).astype(BF16)
        part = _dot(oh, wuv_ref[h])
        out = part if out is None else out + part
    o_ref[...] = out


def _uv(olat, wuv, tm):
    n_seq, rows, _ = olat.shape
    dec = rows // HEADS
    n = n_seq * dec
    return pl.pallas_call(
        functools.partial(_uv_kernel, dec=dec),
        grid=(n // tm,),
        in_specs=[pl.BlockSpec((tm // dec, rows, KV_RANK), lambda i: (i, 0, 0)), _const_spec(wuv.shape)],
        out_specs=pl.BlockSpec((tm, WIDTH), lambda i: (i, 0)),
        out_shape=jax.ShapeDtypeStruct((n, WIDTH), F32),
        compiler_params=pltpu.CompilerParams(dimension_semantics=("arbitrary",), vmem_limit_bytes=VMEM_LIMIT),
        name="uv",
    )(olat, wuv)


def _finish_kernel(x_ref, omla_ref, osb_ref, gmla_ref, gsb_ref, mmla_ref, msb_ref,
                   womla_ref, wosb_ref, wout_ref, y_ref):
    a = (omla_ref[...] * gmla_ref[...]).astype(BF16)
    b = (osb_ref[...] * gsb_ref[...]).astype(BF16)
    merged = mmla_ref[...] * _dot(a, womla_ref[...]) + msb_ref[...] * _dot(b, wosb_ref[...])
    y_ref[...] = x_ref[...] + _dot(merged.astype(BF16), wout_ref[...])


def _finish(x, omla, osb, gmla, gsb, mmla, msb, womla, wosb, wout, tm):
    n = x.shape[0]
    row = lambda w: pl.BlockSpec((tm, w), lambda i: (i, 0))
    return pl.pallas_call(
        _finish_kernel,
        grid=(n // tm,),
        in_specs=[row(D_MODEL), row(WIDTH), row(WIDTH), row(WIDTH), row(WIDTH), row(D_MODEL), row(D_MODEL),
                  _const_spec(womla.shape), _const_spec(wosb.shape), _const_spec(wout.shape)],
        out_specs=row(D_MODEL),
        out_shape=jax.ShapeDtypeStruct((n, D_MODEL), F32),
        compiler_params=pltpu.CompilerParams(dimension_semantics=("arbitrary",), vmem_limit_bytes=VMEM_LIMIT),
        name="finish",
    )(x, omla, osb, gmla, gsb, mmla, msb, womla, wosb, wout)


def _rope_tables(pos):
    inv_freq = ROPE_BASE ** (-jnp.arange(HALF_ROPE, dtype=F32) / HALF_ROPE)
    ang = pos.astype(F32)[:, None] * inv_freq[None, :]
    cos, sin = jnp.cos(ang), jnp.sin(ang)
    zeros = jnp.zeros((pos.shape[0], LANES - QK_ROPE), F32)
    return (jnp.tile(cos, (1, HEADS)), jnp.tile(sin, (1, HEADS)),
            jnp.concatenate([cos, cos, zeros], axis=1), jnp.concatenate([-sin, sin, zeros], axis=1))


def _rope_perm():
    p = np.zeros((2 * LANES, HEADS * LANES), np.float32)
    for h in range(HEADS):
        for d in range(HALF_ROPE):
            p[h * HALF_ROPE + d, h * LANES + d] = 1.0
            p[LANES + h * HALF_ROPE + d, h * LANES + HALF_ROPE + d] = 1.0
    return jnp.asarray(p, BF16)


def _pack_weights(norm_w, w_in, mla_q_a_norm_w, mla_w_uq, mla_q_head_norm_w, mla_w_uk,
                  mla_kv_a_norm_w, mla_k_rope_norm_w):
    sizes = (Q_RANK, KV_RANK, QK_ROPE, WIDTH, WIDTH, WIDTH, WIDTH, WIDTH, D_MODEL, D_MODEL)
    parts, start = [], 0
    for size in sizes:
        parts.append(w_in[:, start:start + size])
        start += size
    q_a, c_kv, k_r, sb_q, sb_k, sb_v, g_mla, g_sb, m_mla, m_sb = parts
    w1 = jnp.concatenate([q_a, c_kv, sb_q, sb_k, sb_v, g_mla, g_sb, m_mla, m_sb, k_r,
                          jnp.zeros((D_MODEL, LANES - QK_ROPE), F32)], axis=1).astype(BF16)
    wuq = mla_w_uq.reshape(Q_RANK, HEADS, Q_HEAD)
    wuq = jnp.concatenate([wuq[:, :, :QK_NOPE].reshape(Q_RANK, -1),
                           wuq[:, :, QK_NOPE:QK_NOPE + HALF_ROPE].reshape(Q_RANK, -1),
                           wuq[:, :, QK_NOPE + HALF_ROPE:].reshape(Q_RANK, -1)], axis=1).astype(BF16)
    g = mla_q_head_norm_w
    gq = jnp.concatenate([jnp.tile(g[:QK_NOPE], HEADS), jnp.tile(g[QK_NOPE:QK_NOPE + HALF_ROPE], HEADS),
                          jnp.tile(g[QK_NOPE + HALF_ROPE:], HEADS)])[None, :]
    wuk_t = jnp.transpose(mla_w_uk, (1, 2, 0))
    zeros = jnp.zeros_like(wuk_t)
    odd = (jnp.arange(HEADS) % 2 == 1)[:, None, None]
    wuk = jnp.concatenate([jnp.where(odd, zeros, wuk_t), jnp.where(odd, wuk_t, zeros)], axis=1).astype(BF16)
    krnw = jnp.concatenate([mla_k_rope_norm_w, jnp.zeros((LANES - QK_ROPE,), F32)])[None, :]
    return [norm_w[None, :], w1, mla_q_a_norm_w[None, :], wuq, gq, wuk, _rope_perm(),
            mla_kv_a_norm_w[None, :], krnw]


def _pack_wuv(mla_w_uv):
    tiled = jnp.tile(jnp.transpose(mla_w_uv, (1, 0, 2)), (1, 1, HEADS))
    own = (jnp.arange(WIDTH) // V_HEAD)[None, None, :] == jnp.arange(HEADS)[:, None, None]
    return jnp.where(own, tiled, 0.0).astype(BF16)


def _tile(n, cap):
    t = cap
    while n % t:
        t //= 2
    return t


def kernel(x_prompt, x_sample, cache_mla_kv, cache_sb_k, cache_sb_v, page_table, norm_w, w_in,
           mla_q_a_norm_w, mla_w_uq, mla_q_head_norm_w, mla_w_uk, mla_kv_a_norm_w, mla_k_rope_norm_w,
           mla_w_uv, w_o_mla, w_o_sb, w_out):
    batch, seq, _ = x_prompt.shape
    n_seq, dec, _ = x_sample.shape
    n_pages = page_table.shape[1]
    past = n_pages * PAGE
    assert n_pages > SB_PREFETCH and dec == 8 and HEADS == 8 and seq % 512 == 0

    weights = _pack_weights(norm_w, w_in, mla_q_a_norm_w, mla_w_uq, mla_q_head_norm_w, mla_w_uk,
                            mla_kv_a_norm_w, mla_k_rope_norm_w)
    wuv = _pack_wuv(mla_w_uv)
    womla, wosb, wout = w_o_mla.astype(BF16), w_o_sb.astype(BF16), w_out.astype(BF16)

    tq = _tile(seq, 256)
    tm = _tile(seq, 256)
    xp = x_prompt.reshape(batch * seq, D_MODEL)
    tabs = _rope_tables(jnp.arange(seq, dtype=jnp.int32))
    ntile = seq // tm
    (mlakv_p, sbk_p, sbv_p, qmla, kvbf, sbq, sbkbf, sbvbf, gmla, gsb, mmla, msb) = _proj(
        xp, tabs, lambda i: (i % ntile, 0), weights, tm, groups=batch)
    tri = _tri2(tq)
    omla = _mla_prompt(qmla, kvbf, wuv, batch, seq, tq)
    osb = _sb_prompt(sbq, sbkbf, sbvbf, tri, batch, seq, tq)
    y_prompt = _finish(xp, omla, osb, gmla, gsb, mmla, msb, womla, wosb, wout, tm)

    ns = n_seq * dec
    tms = _tile(ns, 256)
    xs = x_sample.reshape(ns, D_MODEL)
    tabs_s = [jnp.tile(t, (tms // dec, 1)) for t in _rope_tables(past + jnp.arange(dec, dtype=jnp.int32))]
    (mlakv_s, sbk_s, sbv_s, qmla_s, _, sbq_s, _, _, gmla_s, gsb_s, mmla_s, msb_s) = _proj(
        xs, tabs_s, lambda i: (0, 0), weights, tms)
    tri_s = _tri2(SB_AHEAD + PAGE)
    per_seq = lambda a: a.reshape(n_seq, dec, a.shape[-1])
    pool = cache_mla_kv.shape[0]
    cache_mla_t = jnp.transpose(cache_mla_kv, (0, 2, 1))
    cache_sbk_t = jnp.transpose(cache_sb_k, (0, 2, 3, 1)).reshape(pool, WIDTH, PAGE)
    cache_sbv_t = jnp.transpose(cache_sb_v, (0, 2, 3, 1)).reshape(pool, WIDTH, PAGE)
    olat_s, osb_s = _decode(page_table, per_seq(qmla_s), per_seq(mlakv_s), per_seq(sbq_s),
                            per_seq(sbk_s), per_seq(sbv_s), tri_s, _tri2(PAGE), cache_mla_t, cache_sbk_t, cache_sbv_t)
    omla_s = _uv(olat_s, wuv, tms)
    y_sample = _finish(xs, omla_s, osb_s.reshape(ns, WIDTH), gmla_s, gsb_s, mmla_s, msb_s,
                       womla, wosb, wout, tms)

    heads = lambda a, b, t: a.reshape(b, t, HEADS, SB_DIM)
    heads_t = lambda a: jnp.transpose(a.reshape(batch, HEADS, SB_DIM, seq), (0, 3, 1, 2))
    return (y_prompt.reshape(batch, seq, D_MODEL), y_sample.reshape(n_seq, dec, D_MODEL),
            jnp.transpose(mlakv_p, (0, 2, 1)), heads_t(sbk_p), heads_t(sbv_p),
            mlakv_s.reshape(n_seq, dec, MLA_ROW), heads(sbk_s, n_seq, dec), heads(sbv_s, n_seq, dec))
```

```python
import functools

import numpy as np
import jax
import jax.numpy as jnp
from jax import lax
from jax.experimental import pallas as pl
from jax.experimental.pallas import tpu as pltpu

F32 = jnp.float32
BF16 = jnp.bfloat16

D_MODEL = 1024
HEADS = 8
QK_NOPE = 64
QK_ROPE = 32
HALF_ROPE = QK_ROPE // 2
Q_HEAD = QK_NOPE + QK_ROPE
V_HEAD = 64
KV_RANK = 256
Q_RANK = 768
SB_DIM = 64
WIDTH = HEADS * V_HEAD
MLA_ROW = KV_RANK + QK_ROPE
PAGE = 128
ROPE_BASE = 10000.0
RMS_EPS = 1e-6
MLA_SCALE = Q_HEAD ** -0.5
SB_SCALE = SB_DIM ** -0.5

LANES = 128
QPAD = 3 * LANES
LOG2E = 1.4426950408889634
Q_SCALE = MLA_SCALE * LOG2E
SBQ_SCALE = SB_SCALE * LOG2E
SB_DEAD = -151.0
VMEM_LIMIT = 56 * 1024 * 1024

C_QA = 0
C_CKV = C_QA + Q_RANK
C_SBQ = C_CKV + KV_RANK
C_SBK = C_SBQ + WIDTH
C_SBV = C_SBK + WIDTH
C_GMLA = C_SBV + WIDTH
C_GSB = C_GMLA + WIDTH
C_MMLA = C_GSB + WIDTH
C_MSB = C_MMLA + D_MODEL
C_KR = C_MSB + D_MODEL
C_END = C_KR + LANES


def _rms(x, n):
    return x * lax.rsqrt(jnp.sum(x * x, axis=-1, keepdims=True) * (1.0 / n) + RMS_EPS)


def _sigmoid(x):
    return 1.0 / (1.0 + jnp.exp(-x))


def _split_bf16(x):
    hi = x.astype(BF16)
    lo = (x - hi.astype(F32)).astype(BF16)
    return hi, lo


def _dot(a, b):
    return jnp.dot(a, b, preferred_element_type=F32)


def _dot_nt(a, b):
    return lax.dot_general(a, b, (((1,), (1,)), ((), ())), preferred_element_type=F32)


def _proj_kernel(x_ref, cq_ref, sq_ref, ck_ref, sk_ref, normw_ref, wa_ref, wb_ref, wc_ref, qanw_ref,
                 wuq_ref, gq_ref, wuk_ref, perm_ref, kvnw_ref, krnw_ref,
                 mlakv_ref, sbk_ref, sbv_ref, qmla_ref, kvbf_ref, sbq_ref, sbkbf_ref,
                 sbvbf_ref, gmla_ref, gsb_ref, mmla_ref, msb_ref, *, transposed):
    x = x_ref[...]
    xn = (_rms(x, D_MODEL) * normw_ref[...]).astype(BF16)

    def proj(a, b):
        for ref, lo, hi in ((wa_ref, C_QA, C_SBQ), (wb_ref, C_SBQ, C_KR), (wc_ref, C_KR, C_END)):
            if lo <= a and b <= hi:
                return _dot(xn, ref[:, a - lo:b - lo])
        raise ValueError((a, b))

    q_a = proj(C_QA, C_CKV)
    qan = (_rms(q_a, Q_RANK) * qanw_ref[...]).astype(BF16)
    q = _dot(qan, wuq_ref[...])
    q2 = q * q
    nope_w = HEADS * QK_NOPE
    rope2 = q2[:, nope_w:nope_w + LANES] + q2[:, nope_w + LANES:]
    lane = lax.broadcasted_iota(jnp.int32, rope2.shape, 1)
    upper = lane >= QK_NOPE
    rope_head = lane >> (HALF_ROPE.bit_length() - 1)
    inv = []
    for h in range(HEADS):
        grp = q2[:, LANES * (h // 2):LANES * (h // 2 + 1)]
        mine = jnp.where(upper if h % 2 else ~upper, grp, 0.0) + jnp.where(rope_head == h, rope2, 0.0)
        inv.append(lax.rsqrt(jnp.sum(mine, axis=-1, keepdims=True) * (1.0 / Q_HEAD) + RMS_EPS))
    rope_inv = inv[0]
    for h in range(1, HEADS):
        rope_inv = jnp.where(rope_head == h, inv[h], rope_inv)
    scale = jnp.concatenate([jnp.where(upper, inv[2 * g + 1], inv[2 * g]) for g in range(HEADS // 2)]
                            + [rope_inv, rope_inv], axis=1)
    qn = q * scale * gq_ref[...]
    x1 = qn[:, nope_w:nope_w + LANES]
    x2 = qn[:, nope_w + LANES:]
    cq = cq_ref[...]
    sq = sq_ref[...]
    rot = (jnp.concatenate([x1 * cq - x2 * sq, x2 * cq + x1 * sq], axis=1) * Q_SCALE).astype(BF16)
    rope = _dot(rot, perm_ref[...]).astype(BF16)
    for h in range(HEADS):
        pair = qn[:, LANES * (h // 2):LANES * (h // 2 + 1)].astype(BF16)
        q_lat = _dot(pair, wuk_ref[h]) * Q_SCALE
        qmla_ref[:, QPAD * h:QPAD * h + KV_RANK] = q_lat.astype(BF16)
        qmla_ref[:, QPAD * h + KV_RANK:QPAD * (h + 1)] = rope[:, LANES * h:LANES * (h + 1)]

    cn = _rms(proj(C_CKV, C_SBQ), KV_RANK) * kvnw_ref[...]
    k_r = proj(C_KR, C_END)
    krn = _rms(k_r, QK_ROPE) * krnw_ref[...]
    lane = lax.broadcasted_iota(jnp.int32, krn.shape, 1)
    swapped = jnp.where(lane < HALF_ROPE, pltpu.roll(krn, LANES - HALF_ROPE, 1), pltpu.roll(krn, HALF_ROPE, 1))
    kro = krn * ck_ref[...] + swapped * sk_ref[...]
    if transposed:
        mlakv_ref[0, :KV_RANK, :] = cn.T
        mlakv_ref[0, KV_RANK:, :] = kro.T[:QK_ROPE]
    else:
        mlakv_ref[:, :KV_RANK] = cn
        mlakv_ref[:, KV_RANK:] = kro[:, :QK_ROPE]
    kvbf_ref[:, :KV_RANK] = cn.astype(BF16)
    kvbf_ref[:, KV_RANK:] = kro.astype(BF16)

    sbq_ref[...] = (proj(C_SBQ, C_SBK) * SBQ_SCALE).astype(BF16)
    for c0, f32_ref, bf_ref in ((C_SBK, sbk_ref, sbkbf_ref), (C_SBV, sbv_ref, sbvbf_ref)):
        val = proj(c0, c0 + WIDTH)
        if transposed:
            f32_ref[0] = val.T
        else:
            f32_ref[...] = val
        bf_ref[...] = val.astype(BF16)

    g = proj(C_GMLA, C_GSB)
    gmla_ref[...] = (g * _sigmoid(g)).astype(BF16)
    g = proj(C_GSB, C_MMLA)
    gsb_ref[...] = (g * _sigmoid(g)).astype(BF16)
    mmla_ref[...] = _sigmoid(proj(C_MMLA, C_MSB)).astype(BF16)
    msb_ref[...] = _sigmoid(proj(C_MSB, C_KR)).astype(BF16)


def _const_spec(shape):
    nd = len(shape)
    return pl.BlockSpec(shape, lambda *_: (0,) * nd, pipeline_mode=pl.Buffered(1))


def _proj(x, tables, table_map, weights, tm, groups=None):
    n = x.shape[0]
    row = lambda w: pl.BlockSpec((tm, w), lambda i: (i, 0))
    tab = pl.BlockSpec((tm, LANES), table_map)
    widths = [(MLA_ROW, F32), (WIDTH, F32), (WIDTH, F32), (HEADS * QPAD, BF16), (QPAD, BF16),
              (WIDTH, BF16), (WIDTH, BF16), (WIDTH, BF16), (WIDTH, BF16), (WIDTH, BF16),
              (D_MODEL, BF16), (D_MODEL, BF16)]
    out_specs = [row(w) for w, _ in widths]
    out_shape = [jax.ShapeDtypeStruct((n, w), dt) for w, dt in widths]
    if groups is not None:
        per = n // groups
        ntile = per // tm
        for o in range(3):
            w = widths[o][0]
            out_specs[o] = pl.BlockSpec((1, w, tm), lambda i: (i // ntile, 0, i % ntile))
            out_shape[o] = jax.ShapeDtypeStruct((groups, w, per), F32)
    return pl.pallas_call(
        functools.partial(_proj_kernel, transposed=groups is not None),
        grid=(n // tm,),
        in_specs=[row(D_MODEL), tab, tab, tab, tab] + [_const_spec(w.shape) for w in weights],
        out_specs=out_specs,
        out_shape=out_shape,
        compiler_params=pltpu.CompilerParams(dimension_semantics=("arbitrary",), vmem_limit_bytes=VMEM_LIMIT),
        name="proj",
    )(x, *tables, *weights)


def _mla_prompt_kernel(q_ref, kv_ref, wuv_ref, o_ref, s_scr, m_scr, l_scr, acc_scr, *, tq):
    i = pl.program_id(1)
    tk = 2 * tq
    q = q_ref[...]
    qs = jnp.concatenate([q[:, QPAD * h:QPAD * (h + 1)] for h in range(HEADS)], axis=0)
    m_scr[...] = jnp.full(m_scr.shape, -jnp.inf, F32)
    l_scr[...] = jnp.zeros(l_scr.shape, F32)
    acc_scr[...] = jnp.zeros(acc_scr.shape, F32)
    last = i >> 1

    def keys(j):
        return kv_ref[pl.ds(pl.multiple_of(j * tk, tk), tk), :]

    def scores(buf, j):
        s_scr[buf] = _dot_nt(qs, keys(j))

    def update(buf, j, diagonal):
        s = s_scr[buf]
        if diagonal:
            t = lax.broadcasted_iota(jnp.int32, s.shape, 0) & (tq - 1)
            col = lax.broadcasted_iota(jnp.int32, s.shape, 1)
            s = jnp.where(col <= t + (i & 1) * tq, s, -jnp.inf)
        m_prev = m_scr[...]
        m_new = jnp.maximum(m_prev, jnp.max(s, axis=-1, keepdims=True))
        alpha = jnp.exp2(m_prev - m_new)
        p = jnp.exp2(s - jnp.tile(m_new, (1, tk // LANES)))
        l_scr[...] = alpha * l_scr[...] + jnp.sum(p, axis=-1, keepdims=True)
        acc_scr[...] = (jnp.tile(alpha, (1, KV_RANK // LANES)) * acc_scr[...]
                        + _dot(p.astype(BF16), keys(j)[:, :KV_RANK]))
        m_scr[...] = m_new

    scores(0, 0)

    def body(jj, carry):
        j = 2 * jj
        scores(1, j + 1)
        update(0, j, False)
        scores(0, j + 2)
        update(1, j + 1, False)
        return carry

    lax.fori_loop(0, last >> 1, body, 0)

    @pl.when((last & 1) == 0)
    def _():
        update(0, last, True)

    @pl.when((last & 1) == 1)
    def _():
        scores(1, last)
        update(0, last - 1, False)
        update(1, last, True)

    o_lat = (acc_scr[...] * jnp.tile(1.0 / l_scr[...], (1, KV_RANK // LANES))).astype(BF16)
    out = _dot(o_lat[:tq], wuv_ref[0])
    for h in range(1, HEADS):
        out += _dot(o_lat[h * tq:(h + 1) * tq], wuv_ref[h])
    o_ref[...] = out


def _mla_prompt(qmla, kvbf, wuv, batch, seq, tq):
    nq = seq // tq
    rows = HEADS * tq
    return pl.pallas_call(
        functools.partial(_mla_prompt_kernel, tq=tq),
        grid=(batch, nq),
        in_specs=[pl.BlockSpec((tq, HEADS * QPAD), lambda b, i: (b * nq + i, 0)),
                  pl.BlockSpec((seq, QPAD), lambda b, i: (b, 0)),
                  _const_spec(wuv.shape)],
        out_specs=pl.BlockSpec((tq, WIDTH), lambda b, i: (b * nq + i, 0)),
        out_shape=jax.ShapeDtypeStruct((batch * seq, WIDTH), F32),
        scratch_shapes=[pltpu.VMEM((2, rows, 2 * tq), F32), pltpu.VMEM((rows, LANES), F32),
                        pltpu.VMEM((rows, LANES), F32), pltpu.VMEM((rows, KV_RANK), F32)],
        compiler_params=pltpu.CompilerParams(dimension_semantics=("arbitrary", "arbitrary"),
                                             vmem_limit_bytes=VMEM_LIMIT),
        name="mla_prompt",
    )(qmla, kvbf, wuv)


def _tri2(tk):
    tri = np.tril(np.ones((tk, tk), np.float32))
    return jnp.asarray(np.concatenate([tri, tri], axis=0), BF16)


def _sb_weights(z, carry, tri2, mask):
    nz = -z
    lk = jnp.minimum(nz, 0.0) - jnp.log2(1.0 + jnp.exp2(jnp.minimum(z, nz)))
    if mask is not None:
        lk = jnp.where(mask, lk, 0.0)
    hi, lo = _split_bf16(lk)
    arg = z + _dot(jnp.concatenate([hi, lo], axis=1), tri2)
    total = jnp.sum(lk, axis=-1, keepdims=True)
    if carry is None:
        total = jnp.broadcast_to(total, (z.shape[0], LANES))
    else:
        arg = arg + jnp.tile(carry, (1, z.shape[1] // LANES))
        total = carry + total
    a = jnp.exp2(arg)
    if mask is not None:
        a = jnp.where(mask, a, 0.0)
    return a.astype(BF16), total


def _sb_prompt_kernel(q_ref, k_ref, v_ref, tri_ref, o_ref, carry_scr, acc_scr, *, tq):
    i = pl.program_id(1)
    q = q_ref[...]
    lane = lax.broadcasted_iota(jnp.int32, (tq, LANES), 1)
    zero = jnp.zeros((tq, LANES), BF16)
    groups = HEADS // 2
    qg = []
    for g in range(groups):
        pair = q[:, LANES * g:LANES * (g + 1)]
        qg.append(jnp.concatenate([jnp.where(lane < SB_DIM, pair, zero), jnp.where(lane >= SB_DIM, pair, zero)],
                                  axis=0))
    carry_scr[...] = jnp.zeros(carry_scr.shape, F32)
    acc_scr[...] = jnp.zeros(acc_scr.shape, F32)
    tri = tri_ref[...]

    def block(j, mask):
        start = pl.multiple_of(j * tq, tq)
        z = jnp.concatenate([_dot_nt(qg[g], k_ref[pl.ds(start, tq), LANES * g:LANES * (g + 1)])
                             for g in range(groups)], axis=0)
        a, carry = _sb_weights(z, carry_scr[...], tri, mask)
        carry_scr[...] = carry
        for g in range(groups):
            acc_scr[g] += _dot(a[2 * tq * g:2 * tq * (g + 1)], v_ref[pl.ds(start, tq), LANES * g:LANES * (g + 1)])

    t = lax.broadcasted_iota(jnp.int32, (HEADS * tq, tq), 0) & (tq - 1)
    col = lax.broadcasted_iota(jnp.int32, (HEADS * tq, tq), 1)
    block(i, col < t)

    def alive():
        return (jnp.max(carry_scr[...]) >= SB_DEAD).astype(jnp.int32)

    def cond(state):
        j, live = state
        return jnp.logical_and(j >= 0, live == 1)

    def body(state):
        j, _ = state
        block(j, None)
        return j - 1, alive()

    lax.while_loop(cond, body, (i - 1, alive()))

    for g in range(groups):
        o_ref[:, LANES * g:LANES * (g + 1)] = jnp.where(lane < SB_DIM, acc_scr[g, :tq], acc_scr[g, tq:])


def _sb_prompt(sbq, sbk, sbv, tri, batch, seq, tq):
    nq = seq // tq
    kv_spec = pl.BlockSpec((seq, WIDTH), lambda b, i: (b, 0), pipeline_mode=pl.Buffered(1))
    return pl.pallas_call(
        functools.partial(_sb_prompt_kernel, tq=tq),
        grid=(batch, nq),
        in_specs=[pl.BlockSpec((tq, WIDTH), lambda b, i: (b * nq + i, 0)), kv_spec, kv_spec,
                  _const_spec(tri.shape)],
        out_specs=pl.BlockSpec((tq, WIDTH), lambda b, i: (b * nq + i, 0)),
        out_shape=jax.ShapeDtypeStruct((batch * seq, WIDTH), F32),
        scratch_shapes=[pltpu.VMEM((HEADS * tq, LANES), F32), pltpu.VMEM((HEADS // 2, 2 * tq, LANES), F32)],
        compiler_params=pltpu.CompilerParams(dimension_semantics=("arbitrary", "arbitrary"),
                                             vmem_limit_bytes=VMEM_LIMIT),
        name="sb_prompt",
    )(sbq, sbk, sbv, tri)


SB_PREFETCH = 2
SB_AHEAD = SB_PREFETCH * PAGE


def _decode_kernel(pt_ref, q_ref, kvnew_ref, sbq_ref, sbknew_ref, sbvnew_ref, tri_ref, tripage_ref,
                   cmla_hbm, csbk_hbm, csbv_hbm, olat_ref, osb_ref,
                   kbuf, sbkbuf, sbvbuf, slowk, slowv, carry_scr, acc_scr,
                   sem_mla, sem_sbk, sem_sbv, sem_slow, *, n_seq, n_pages, dec):
    n = pl.program_id(0)
    slot = n & 1
    rows = HEADS * dec

    def mla_copy(seq, s, p):
        return pltpu.make_async_copy(cmla_hbm.at[pt_ref[seq, p]], kbuf.at[s, p], sem_mla.at[s])

    def sb_copy(seq, s, r, src, dst, sem):
        return pltpu.make_async_copy(src.at[pt_ref[seq, n_pages - 1 - r]], dst.at[s, SB_PREFETCH - 1 - r], sem.at[s])

    def fetch(seq, s):
        for p in range(n_pages):
            mla_copy(seq, s, p).start()
        for r in range(SB_PREFETCH):
            sb_copy(seq, s, r, csbk_hbm, sbkbuf, sem_sbk).start()
            sb_copy(seq, s, r, csbv_hbm, sbvbuf, sem_sbv).start()

    @pl.when(n == 0)
    def _():
        fetch(0, 0)

    @pl.when(n + 1 < n_seq)
    def _():
        fetch(n + 1, 1 - slot)

    lane = lax.broadcasted_iota(jnp.int32, (rows, WIDTH), 1)
    rowi = lax.broadcasted_iota(jnp.int32, (rows, WIDTH), 0)
    own = (lane >> 6) == (rowi >> 3)
    qbd = jnp.where(own, jnp.tile(sbq_ref[0].astype(F32), (HEADS, 1)), 0.0).astype(BF16)
    tri = tri_ref[...]
    pad = jnp.zeros((PAGE - dec, WIDTH), F32)
    knew = jnp.concatenate([sbknew_ref[0], pad], axis=0).astype(BF16)
    vnew = jnp.concatenate([sbvnew_ref[0], pad], axis=0).astype(BF16)

    for r in range(SB_PREFETCH):
        sb_copy(n, slot, r, csbk_hbm, sbkbuf, sem_sbk).wait()
        sb_copy(n, slot, r, csbv_hbm, sbvbuf, sem_sbv).wait()

    for p in range(n_pages):
        pltpu.make_async_copy(cmla_hbm.at[0], kbuf.at[slot, p], sem_mla.at[slot]).wait()

    qf = q_ref[0].astype(F32)
    qs = jnp.concatenate([qf[:, QPAD * h:QPAD * h + MLA_ROW] for h in range(HEADS)], axis=0).astype(BF16)
    kc = jnp.concatenate([kbuf[slot, p].astype(BF16) for p in range(n_pages)], axis=1)
    s = _dot(qs, kc)
    kn = jnp.concatenate([kvnew_ref[0], jnp.zeros((PAGE - dec, MLA_ROW), F32)], axis=0).astype(BF16)
    t_mla = lax.broadcasted_iota(jnp.int32, (rows, PAGE), 0) & (dec - 1)
    col = lax.broadcasted_iota(jnp.int32, (rows, PAGE), 1)
    s_new = jnp.where(col <= t_mla, _dot_nt(qs, kn), -jnp.inf)
    m = jnp.maximum(jnp.max(s, axis=-1, keepdims=True), jnp.max(s_new, axis=-1, keepdims=True))
    p = jnp.exp2(s - m)
    p_new = jnp.exp2(s_new - m)
    l = jnp.sum(p, axis=-1, keepdims=True) + jnp.sum(p_new, axis=-1, keepdims=True)
    o = _dot_nt(p.astype(BF16), kc[:KV_RANK]) + _dot(p_new.astype(BF16), kn[:, :KV_RANK])
    olat_ref[0] = o * (1.0 / l)

    kt = jnp.concatenate([sbkbuf[slot, r].astype(BF16) for r in range(SB_PREFETCH)], axis=1)
    vt = jnp.concatenate([sbvbuf[slot, r].astype(BF16) for r in range(SB_PREFETCH)], axis=1)
    z = jnp.concatenate([_dot(qbd, kt), _dot_nt(qbd, knew)], axis=1)
    t_sb = lax.broadcasted_iota(jnp.int32, z.shape, 0) & (dec - 1)
    col_sb = lax.broadcasted_iota(jnp.int32, z.shape, 1)
    a, carry0 = _sb_weights(z, None, tri, col_sb < SB_AHEAD + t_sb)
    acc0 = _dot_nt(a[:, :SB_AHEAD], vt) + _dot(a[:, SB_AHEAD:], vnew)

    def own_heads(acc):
        sel = jnp.where(own, acc, 0.0)
        out = sel[:dec]
        for h in range(1, HEADS):
            out += sel[h * dec:(h + 1) * dec]
        return out

    osb_ref[0] = own_heads(acc0)

    @pl.when(jnp.max(carry0) >= SB_DEAD)
    def _():
        acc_scr[...] = acc0
        carry_scr[...] = carry0

        def cond(state):
            p, live = state
            return jnp.logical_and(p >= 0, live == 1)

        def body(state):
            p, _ = state
            ck = pltpu.make_async_copy(csbk_hbm.at[pt_ref[n, p]], slowk, sem_slow.at[0])
            cv = pltpu.make_async_copy(csbv_hbm.at[pt_ref[n, p]], slowv, sem_slow.at[1])
            ck.start()
            cv.start()
            ck.wait()
            cv.wait()
            a, c = _sb_weights(_dot(qbd, slowk[...].astype(BF16)), carry_scr[...], tripage_ref[...], None)
            acc_scr[...] += _dot_nt(a, slowv[...].astype(BF16))
            carry_scr[...] = c
            return p - 1, (jnp.max(c) >= SB_DEAD).astype(jnp.int32)

        lax.while_loop(cond, body, (jnp.int32(n_pages - 1 - SB_PREFETCH), jnp.int32(1)))
        osb_ref[0] = own_heads(acc_scr[...])


def _decode(page_table, qmla, kvnew, sbq, sbknew, sbvnew, tri, tripage, cache_mla_t, cache_sbk_t, cache_sbv_t):
    n_seq, n_pages = page_table.shape
    dec = kvnew.shape[1]
    rows = HEADS * dec
    per_seq = lambda r, w: pl.BlockSpec((1, r, w), lambda n, pt: (n, 0, 0))
    const = lambda shape: pl.BlockSpec(shape, lambda n, pt: (0,) * len(shape), pipeline_mode=pl.Buffered(1))
    hbm = pl.BlockSpec(memory_space=pl.ANY)
    grid_spec = pltpu.PrefetchScalarGridSpec(
        num_scalar_prefetch=1,
        grid=(n_seq,),
        in_specs=[per_seq(dec, HEADS * QPAD), per_seq(dec, MLA_ROW), per_seq(dec, WIDTH), per_seq(dec, WIDTH),
                  per_seq(dec, WIDTH), const(tri.shape), const(tripage.shape), hbm, hbm, hbm],
        out_specs=[per_seq(rows, KV_RANK), per_seq(dec, WIDTH)],
        scratch_shapes=[
            pltpu.VMEM((2, n_pages, MLA_ROW, PAGE), F32),
            pltpu.VMEM((2, SB_PREFETCH, WIDTH, PAGE), F32),
            pltpu.VMEM((2, SB_PREFETCH, WIDTH, PAGE), F32),
            pltpu.VMEM((WIDTH, PAGE), F32),
            pltpu.VMEM((WIDTH, PAGE), F32),
            pltpu.VMEM((rows, LANES), F32),
            pltpu.VMEM((rows, WIDTH), F32),
            pltpu.SemaphoreType.DMA((2,)),
            pltpu.SemaphoreType.DMA((2,)),
            pltpu.SemaphoreType.DMA((2,)),
            pltpu.SemaphoreType.DMA((2,)),
        ],
    )
    return pl.pallas_call(
        functools.partial(_decode_kernel, n_seq=n_seq, n_pages=n_pages, dec=dec),
        grid_spec=grid_spec,
        out_shape=[jax.ShapeDtypeStruct((n_seq, rows, KV_RANK), F32), jax.ShapeDtypeStruct((n_seq, dec, WIDTH), F32)],
        compiler_params=pltpu.CompilerParams(dimension_semantics=("arbitrary",), vmem_limit_bytes=VMEM_LIMIT),
        name="decode",
    )(page_table, qmla, kvnew, sbq, sbknew, sbvnew, tri, tripage, cache_mla_t, cache_sbk_t, cache_sbv_t)


def _uv_kernel(olat_ref, wuv_ref, o_ref, *, dec):
    o = olat_ref[...]
    seqs = o.shape[0]
    out = None
    for h in range(HEADS):
        oh = o[:, h * dec:(h + 1) * dec, :].reshape(seqs * dec, KV_RANK).astype(BF16)
        part = _dot(oh, wuv_ref[h])
        out = part if out is None else out + part
    o_ref[...] = out


def _uv(olat, wuv, tm):
    n_seq, rows, _ = olat.shape
    dec = rows // HEADS
    n = n_seq * dec
    return pl.pallas_call(
        functools.partial(_uv_kernel, dec=dec),
        grid=(n // tm,),
        in_specs=[pl.BlockSpec((tm // dec, rows, KV_RANK), lambda i: (i, 0, 0)), _const_spec(wuv.shape)],
        out_specs=pl.BlockSpec((tm, WIDTH), lambda i: (i, 0)),
        out_shape=jax.ShapeDtypeStruct((n, WIDTH), F32),
        compiler_params=pltpu.CompilerParams(dimension_semantics=("arbitrary",), vmem_limit_bytes=VMEM_LIMIT),
        name="uv",
    )(olat, wuv)


def _finish_kernel(x_ref, omla_ref, osb_ref, gmla_ref, gsb_ref, mmla_ref, msb_ref,
                   womla_ref, wosb_ref, wout_ref, y_ref):
    a = (omla_ref[...] * gmla_ref[...]).astype(BF16)
    b = (osb_ref[...] * gsb_ref[...]).astype(BF16)
    merged = mmla_ref[...] * _dot(a, womla_ref[...]) + msb_ref[...] * _dot(b, wosb_ref[...])
    y_ref[...] = x_ref[...] + _dot(merged.astype(BF16), wout_ref[...])


def _finish(x, omla, osb, gmla, gsb, mmla, msb, womla, wosb, wout, tm):
    n = x.shape[0]
    row = lambda w: pl.BlockSpec((tm, w), lambda i: (i, 0))
    return pl.pallas_call(
        _finish_kernel,
        grid=(n // tm,),
        in_specs=[row(D_MODEL), row(WIDTH), row(WIDTH), row(WIDTH), row(WIDTH), row(D_MODEL), row(D_MODEL),
                  _const_spec(womla.shape), _const_spec(wosb.shape), _const_spec(wout.shape)],
        out_specs=row(D_MODEL),
        out_shape=jax.ShapeDtypeStruct((n, D_MODEL), F32),
        compiler_params=pltpu.CompilerParams(dimension_semantics=("arbitrary",), vmem_limit_bytes=VMEM_LIMIT),
        name="finish",
    )(x, omla, osb, gmla, gsb, mmla, msb, womla, wosb, wout)


def _rope_tables(start, count):
    inv_freq = ROPE_BASE ** (-np.arange(HALF_ROPE, dtype=np.float64) / HALF_ROPE)
    ang = np.arange(start, start + count, dtype=np.float64)[:, None] * inv_freq[None, :]
    cos, sin = jnp.asarray(np.cos(ang), F32), jnp.asarray(np.sin(ang), F32)
    zeros = jnp.zeros((count, LANES - QK_ROPE), F32)
    return (jnp.tile(cos, (1, HEADS)), jnp.tile(sin, (1, HEADS)),
            jnp.concatenate([cos, cos, zeros], axis=1), jnp.concatenate([-sin, sin, zeros], axis=1))


def _rope_perm():
    p = np.zeros((2 * LANES, HEADS * LANES), np.float32)
    for h in range(HEADS):
        for d in range(HALF_ROPE):
            p[h * HALF_ROPE + d, h * LANES + d] = 1.0
            p[LANES + h * HALF_ROPE + d, h * LANES + HALF_ROPE + d] = 1.0
    return jnp.asarray(p, BF16)


def _pack_weights(norm_w, w_in, mla_q_a_norm_w, mla_w_uq, mla_q_head_norm_w, mla_w_uk,
                  mla_kv_a_norm_w, mla_k_rope_norm_w):
    kr0 = Q_RANK + KV_RANK
    wa = w_in[:, :kr0].astype(BF16)
    wb = w_in[:, kr0 + QK_ROPE:].astype(BF16)
    wc = jnp.pad(w_in[:, kr0:kr0 + QK_ROPE], ((0, 0), (0, LANES - QK_ROPE))).astype(BF16)
    assert wa.shape[1] == C_SBQ - C_QA and wb.shape[1] == C_KR - C_SBQ
    wuq = mla_w_uq.reshape(Q_RANK, HEADS, Q_HEAD)
    wuq = jnp.concatenate([wuq[:, :, :QK_NOPE].reshape(Q_RANK, -1),
                           wuq[:, :, QK_NOPE:QK_NOPE + HALF_ROPE].reshape(Q_RANK, -1),
                           wuq[:, :, QK_NOPE + HALF_ROPE:].reshape(Q_RANK, -1)], axis=1).astype(BF16)
    g = mla_q_head_norm_w
    gq = jnp.concatenate([jnp.tile(g[:QK_NOPE], HEADS), jnp.tile(g[QK_NOPE:QK_NOPE + HALF_ROPE], HEADS),
                          jnp.tile(g[QK_NOPE + HALF_ROPE:], HEADS)])[None, :]
    wuk_t = jnp.transpose(mla_w_uk, (1, 2, 0))
    zeros = jnp.zeros_like(wuk_t)
    odd = (jnp.arange(HEADS) % 2 == 1)[:, None, None]
    wuk = jnp.concatenate([jnp.where(odd, zeros, wuk_t), jnp.where(odd, wuk_t, zeros)], axis=1).astype(BF16)
    krnw = jnp.concatenate([mla_k_rope_norm_w, jnp.zeros((LANES - QK_ROPE,), F32)])[None, :]
    return [norm_w[None, :], wa, wb, wc, mla_q_a_norm_w[None, :], wuq, gq, wuk, _rope_perm(),
            mla_kv_a_norm_w[None, :], krnw]


def _pack_wuv(mla_w_uv):
    tiled = jnp.tile(jnp.transpose(mla_w_uv, (1, 0, 2)), (1, 1, HEADS))
    own = (jnp.arange(WIDTH) // V_HEAD)[None, None, :] == jnp.arange(HEADS)[:, None, None]
    return jnp.where(own, tiled, 0.0).astype(BF16)


def _tile(n, cap):
    t = cap
    while n % t:
        t //= 2
    return t


def kernel(x_prompt, x_sample, cache_mla_kv, cache_sb_k, cache_sb_v, page_table, norm_w, w_in,
           mla_q_a_norm_w, mla_w_uq, mla_q_head_norm_w, mla_w_uk, mla_kv_a_norm_w, mla_k_rope_norm_w,
           mla_w_uv, w_o_mla, w_o_sb, w_out):
    batch, seq, _ = x_prompt.shape
    n_seq, dec, _ = x_sample.shape
    n_pages = page_table.shape[1]
    past = n_pages * PAGE
    assert n_pages > SB_PREFETCH and dec == 8 and HEADS == 8 and seq % 512 == 0

    weights = _pack_weights(norm_w, w_in, mla_q_a_norm_w, mla_w_uq, mla_q_head_norm_w, mla_w_uk,
                            mla_kv_a_norm_w, mla_k_rope_norm_w)
    wuv = _pack_wuv(mla_w_uv)
    womla, wosb, wout = w_o_mla.astype(BF16), w_o_sb.astype(BF16), w_out.astype(BF16)

    tq = _tile(seq, 256)
    tm = _tile(seq, 256)
    xp = x_prompt.reshape(batch * seq, D_MODEL)
    tabs = _rope_tables(0, seq)
    ntile = seq // tm
    (mlakv_p, sbk_p, sbv_p, qmla, kvbf, sbq, sbkbf, sbvbf, gmla, gsb, mmla, msb) = _proj(
        xp, tabs, lambda i: (i % ntile, 0), weights, tm, groups=batch)
    tri = _tri2(tq)
    omla = _mla_prompt(qmla, kvbf, wuv, batch, seq, tq)
    osb = _sb_prompt(sbq, sbkbf, sbvbf, tri, batch, seq, tq)
    y_prompt = _finish(xp, omla, osb, gmla, gsb, mmla, msb, womla, wosb, wout, _tile(seq, 512))

    ns = n_seq * dec
    tms = _tile(ns, 256)
    xs = x_sample.reshape(ns, D_MODEL)
    tabs_s = [jnp.tile(t, (tms // dec, 1)) for t in _rope_tables(past, dec)]
    (mlakv_s, sbk_s, sbv_s, qmla_s, _, sbq_s, _, _, gmla_s, gsb_s, mmla_s, msb_s) = _proj(
        xs, tabs_s, lambda i: (0, 0), weights, tms)
    tri_s = _tri2(SB_AHEAD + PAGE)
    per_seq = lambda a: a.reshape(n_seq, dec, a.shape[-1])
    pool = cache_mla_kv.shape[0]
    cache_mla_t = jnp.transpose(cache_mla_kv, (0, 2, 1))
    cache_sbk_t = jnp.transpose(cache_sb_k, (0, 2, 3, 1)).reshape(pool, WIDTH, PAGE)
    cache_sbv_t = jnp.transpose(cache_sb_v, (0, 2, 3, 1)).reshape(pool, WIDTH, PAGE)
    olat_s, osb_s = _decode(page_table, per_seq(qmla_s), per_seq(mlakv_s), per_seq(sbq_s),
                            per_seq(sbk_s), per_seq(sbv_s), tri_s, _tri2(PAGE), cache_mla_t, cache_sbk_t, cache_sbv_t)
    omla_s = _uv(olat_s, wuv, tms)
    y_sample = _finish(xs, omla_s, osb_s.reshape(ns, WIDTH), gmla_s, gsb_s, mmla_s, msb_s,
                       womla, wosb, wout, tms)

    heads = lambda a, b, t: a.reshape(b, t, HEADS, SB_DIM)
    heads_t = lambda a: jnp.transpose(a.reshape(batch, HEADS, SB_DIM, seq), (0, 3, 1, 2))
    return (y_prompt.reshape(batch, seq, D_MODEL), y_sample.reshape(n_seq, dec, D_MODEL),
            jnp.transpose(mlakv_p, (0, 2, 1)), heads_t(sbk_p), heads_t(sbv_p),
            mlakv_s.reshape(n_seq, dec, MLA_ROW), heads(sbk_s, n_seq, dec), heads(sbv_s, n_seq, dec))
```

```python
import functools

import numpy as np
import jax
import jax.numpy as jnp
from jax import lax
from jax.experimental import pallas as pl
from jax.experimental.pallas import tpu as pltpu

F32 = jnp.float32
BF16 = jnp.bfloat16

D_MODEL = 1024
HEADS = 8
QK_NOPE = 64
QK_ROPE = 32
HALF_ROPE = QK_ROPE // 2
Q_HEAD = QK_NOPE + QK_ROPE
V_HEAD = 64
KV_RANK = 256
Q_RANK = 768
SB_DIM = 64
WIDTH = HEADS * V_HEAD
MLA_ROW = KV_RANK + QK_ROPE
PAGE = 128
ROPE_BASE = 10000.0
RMS_EPS = 1e-6
MLA_SCALE = Q_HEAD ** -0.5
SB_SCALE = SB_DIM ** -0.5

LANES = 128
QPAD = 3 * LANES
LOG2E = 1.4426950408889634
Q_SCALE = MLA_SCALE * LOG2E
SBQ_SCALE = SB_SCALE * LOG2E
SB_DEAD = -151.0
VMEM_LIMIT = 56 * 1024 * 1024

C_QA = 0
C_CKV = C_QA + Q_RANK
C_SBQ = C_CKV + KV_RANK
C_SBK = C_SBQ + WIDTH
C_SBV = C_SBK + WIDTH
C_GMLA = C_SBV + WIDTH
C_GSB = C_GMLA + WIDTH
C_MMLA = C_GSB + WIDTH
C_MSB = C_MMLA + D_MODEL
C_KR = C_MSB + D_MODEL
C_END = C_KR + LANES


def _rms(x, n):
    return x * lax.rsqrt(jnp.sum(x * x, axis=-1, keepdims=True) * (1.0 / n) + RMS_EPS)


def _sigmoid(x):
    return 1.0 / (1.0 + jnp.exp(-x))


def _split_bf16(x):
    hi = x.astype(BF16)
    lo = (x - hi.astype(F32)).astype(BF16)
    return hi, lo


def _dot(a, b):
    return jnp.dot(a, b, preferred_element_type=F32)


def _dot_nt(a, b):
    return lax.dot_general(a, b, (((1,), (1,)), ((), ())), preferred_element_type=F32)


def _proj_kernel(x_ref, cq_ref, sq_ref, ck_ref, sk_ref, normw_ref, wa_ref, wb_ref, wc_ref, qanw_ref,
                 wuq_ref, gq_ref, wuk_ref, perm_ref, kvnw_ref, krnw_ref,
                 mlakv_ref, sbk_ref, sbv_ref, qmla_ref, kvbf_ref, sbq_ref, sbkbf_ref,
                 sbvbf_ref, gmla_ref, gsb_ref, mmla_ref, msb_ref, *, transposed):
    x = x_ref[...]
    xn = (_rms(x, D_MODEL) * normw_ref[...]).astype(BF16)

    def proj(a, b):
        for ref, lo, hi in ((wa_ref, C_QA, C_SBQ), (wb_ref, C_SBQ, C_KR), (wc_ref, C_KR, C_END)):
            if lo <= a and b <= hi:
                return _dot(xn, ref[:, a - lo:b - lo])
        raise ValueError((a, b))

    q_a = proj(C_QA, C_CKV)
    qan = (_rms(q_a, Q_RANK) * qanw_ref[...]).astype(BF16)
    q = _dot(qan, wuq_ref[...])
    q2 = q * q
    nope_w = HEADS * QK_NOPE
    rope2 = q2[:, nope_w:nope_w + LANES] + q2[:, nope_w + LANES:]
    lane = lax.broadcasted_iota(jnp.int32, rope2.shape, 1)
    upper = lane >= QK_NOPE
    rope_head = lane >> (HALF_ROPE.bit_length() - 1)
    inv = []
    for h in range(HEADS):
        grp = q2[:, LANES * (h // 2):LANES * (h // 2 + 1)]
        mine = jnp.where(upper if h % 2 else ~upper, grp, 0.0) + jnp.where(rope_head == h, rope2, 0.0)
        inv.append(lax.rsqrt(jnp.sum(mine, axis=-1, keepdims=True) * (1.0 / Q_HEAD) + RMS_EPS))
    rope_inv = inv[0]
    for h in range(1, HEADS):
        rope_inv = jnp.where(rope_head == h, inv[h], rope_inv)
    scale = jnp.concatenate([jnp.where(upper, inv[2 * g + 1], inv[2 * g]) for g in range(HEADS // 2)]
                            + [rope_inv, rope_inv], axis=1)
    qn = q * scale * gq_ref[...]
    x1 = qn[:, nope_w:nope_w + LANES]
    x2 = qn[:, nope_w + LANES:]
    cq = cq_ref[...]
    sq = sq_ref[...]
    rot = (jnp.concatenate([x1 * cq - x2 * sq, x2 * cq + x1 * sq], axis=1) * Q_SCALE).astype(BF16)
    rope = _dot(rot, perm_ref[...]).astype(BF16)
    for h in range(HEADS):
        pair = qn[:, LANES * (h // 2):LANES * (h // 2 + 1)].astype(BF16)
        q_lat = _dot(pair, wuk_ref[h]) * Q_SCALE
        qmla_ref[:, QPAD * h:QPAD * h + KV_RANK] = q_lat.astype(BF16)
        qmla_ref[:, QPAD * h + KV_RANK:QPAD * (h + 1)] = rope[:, LANES * h:LANES * (h + 1)]

    cn = _rms(proj(C_CKV, C_SBQ), KV_RANK) * kvnw_ref[...]
    k_r = proj(C_KR, C_END)
    krn = _rms(k_r, QK_ROPE) * krnw_ref[...]
    lane = lax.broadcasted_iota(jnp.int32, krn.shape, 1)
    swapped = jnp.where(lane < HALF_ROPE, pltpu.roll(krn, LANES - HALF_ROPE, 1), pltpu.roll(krn, HALF_ROPE, 1))
    kro = krn * ck_ref[...] + swapped * sk_ref[...]
    if transposed:
        mlakv_ref[0, :KV_RANK, :] = cn.T
        mlakv_ref[0, KV_RANK:, :] = kro.T[:QK_ROPE]
    else:
        mlakv_ref[:, :KV_RANK] = cn
        mlakv_ref[:, KV_RANK:] = kro[:, :QK_ROPE]
    kvbf_ref[:, :KV_RANK] = cn.astype(BF16)
    kvbf_ref[:, KV_RANK:] = kro.astype(BF16)

    sbq_ref[...] = (proj(C_SBQ, C_SBK) * SBQ_SCALE).astype(BF16)
    for c0, f32_ref, bf_ref in ((C_SBK, sbk_ref, sbkbf_ref), (C_SBV, sbv_ref, sbvbf_ref)):
        val = proj(c0, c0 + WIDTH)
        if transposed:
            f32_ref[0] = val.T
        else:
            f32_ref[...] = val
        bf_ref[...] = val.astype(BF16)

    g = proj(C_GMLA, C_GSB)
    gmla_ref[...] = (g * _sigmoid(g)).astype(BF16)
    g = proj(C_GSB, C_MMLA)
    gsb_ref[...] = (g * _sigmoid(g)).astype(BF16)
    mmla_ref[...] = _sigmoid(proj(C_MMLA, C_MSB)).astype(BF16)
    msb_ref[...] = _sigmoid(proj(C_MSB, C_KR)).astype(BF16)


def _const_spec(shape):
    nd = len(shape)
    return pl.BlockSpec(shape, lambda *_: (0,) * nd, pipeline_mode=pl.Buffered(1))


def _proj(x, tables, table_map, weights, tm, groups=None):
    n = x.shape[0]
    row = lambda w: pl.BlockSpec((tm, w), lambda i: (i, 0))
    tab = pl.BlockSpec((tm, LANES), table_map)
    widths = [(MLA_ROW, F32), (WIDTH, F32), (WIDTH, F32), (HEADS * QPAD, BF16), (QPAD, BF16),
              (WIDTH, BF16), (WIDTH, BF16), (WIDTH, BF16), (WIDTH, BF16), (WIDTH, BF16),
              (D_MODEL, BF16), (D_MODEL, BF16)]
    out_specs = [row(w) for w, _ in widths]
    out_shape = [jax.ShapeDtypeStruct((n, w), dt) for w, dt in widths]
    if groups is not None:
        per = n // groups
        ntile = per // tm
        for o in range(3):
            w = widths[o][0]
            out_specs[o] = pl.BlockSpec((1, w, tm), lambda i: (i // ntile, 0, i % ntile))
            out_shape[o] = jax.ShapeDtypeStruct((groups, w, per), F32)
    return pl.pallas_call(
        functools.partial(_proj_kernel, transposed=groups is not None),
        grid=(n // tm,),
        in_specs=[row(D_MODEL), tab, tab, tab, tab] + [_const_spec(w.shape) for w in weights],
        out_specs=out_specs,
        out_shape=out_shape,
        compiler_params=pltpu.CompilerParams(dimension_semantics=("arbitrary",), vmem_limit_bytes=VMEM_LIMIT),
        name="proj",
    )(x, *tables, *weights)


def _mla_prompt_kernel(q_ref, kv_ref, wuv_ref, o_ref, s_scr, m_scr, l_scr, acc_scr, *, tq):
    i = pl.program_id(1)
    tk = 2 * tq
    q = q_ref[...]
    qs = jnp.concatenate([q[:, QPAD * h:QPAD * (h + 1)] for h in range(HEADS)], axis=0)
    m_scr[...] = jnp.full(m_scr.shape, -jnp.inf, F32)
    l_scr[...] = jnp.zeros(l_scr.shape, F32)
    acc_scr[...] = jnp.zeros(acc_scr.shape, F32)
    last = i >> 1

    def keys(j):
        return kv_ref[pl.ds(pl.multiple_of(j * tk, tk), tk), :]

    def scores(buf, j):
        s_scr[buf] = _dot_nt(qs, keys(j))

    def update(buf, j, diagonal):
        s = s_scr[buf]
        if diagonal:
            t = lax.broadcasted_iota(jnp.int32, s.shape, 0) & (tq - 1)
            col = lax.broadcasted_iota(jnp.int32, s.shape, 1)
            s = jnp.where(col <= t + (i & 1) * tq, s, -jnp.inf)
        m_prev = m_scr[...]
        m_new = jnp.maximum(m_prev, jnp.max(s, axis=-1, keepdims=True))
        alpha = jnp.exp2(m_prev - m_new)
        p = jnp.exp2(s - jnp.tile(m_new, (1, tk // LANES)))
        l_scr[...] = alpha * l_scr[...] + jnp.sum(p, axis=-1, keepdims=True)
        acc_scr[...] = (jnp.tile(alpha, (1, KV_RANK // LANES)) * acc_scr[...]
                        + _dot(p.astype(BF16), keys(j)[:, :KV_RANK]))
        m_scr[...] = m_new

    scores(0, 0)

    def body(jj, carry):
        j = 2 * jj
        scores(1, j + 1)
        update(0, j, False)
        scores(0, j + 2)
        update(1, j + 1, False)
        return carry

    lax.fori_loop(0, last >> 1, body, 0)

    @pl.when((last & 1) == 0)
    def _():
        update(0, last, True)

    @pl.when((last & 1) == 1)
    def _():
        scores(1, last)
        update(0, last - 1, False)
        update(1, last, True)

    o_lat = (acc_scr[...] * jnp.tile(1.0 / l_scr[...], (1, KV_RANK // LANES))).astype(BF16)
    out = _dot(o_lat[:tq], wuv_ref[0])
    for h in range(1, HEADS):
        out += _dot(o_lat[h * tq:(h + 1) * tq], wuv_ref[h])
    o_ref[...] = out


def _mla_prompt(qmla, kvbf, wuv, batch, seq, tq):
    nq = seq // tq
    rows = HEADS * tq
    return pl.pallas_call(
        functools.partial(_mla_prompt_kernel, tq=tq),
        grid=(batch, nq),
        in_specs=[pl.BlockSpec((tq, HEADS * QPAD), lambda b, i: (b * nq + i, 0)),
                  pl.BlockSpec((seq, QPAD), lambda b, i: (b, 0)),
                  _const_spec(wuv.shape)],
        out_specs=pl.BlockSpec((tq, WIDTH), lambda b, i: (b * nq + i, 0)),
        out_shape=jax.ShapeDtypeStruct((batch * seq, WIDTH), F32),
        scratch_shapes=[pltpu.VMEM((2, rows, 2 * tq), F32), pltpu.VMEM((rows, LANES), F32),
                        pltpu.VMEM((rows, LANES), F32), pltpu.VMEM((rows, KV_RANK), F32)],
        compiler_params=pltpu.CompilerParams(dimension_semantics=("arbitrary", "arbitrary"),
                                             vmem_limit_bytes=VMEM_LIMIT),
        name="mla_prompt",
    )(qmla, kvbf, wuv)


def _tri2(tk):
    tri = np.tril(np.ones((tk, tk), np.float32))
    return jnp.asarray(np.concatenate([tri, tri], axis=0), BF16)


def _sb_weights(z, carry, tri2, mask):
    nz = -z
    lk = jnp.minimum(nz, 0.0) - jnp.log2(1.0 + jnp.exp2(jnp.minimum(z, nz)))
    if mask is not None:
        lk = jnp.where(mask, lk, 0.0)
    hi, lo = _split_bf16(lk)
    arg = z + _dot(jnp.concatenate([hi, lo], axis=1), tri2)
    total = jnp.sum(lk, axis=-1, keepdims=True)
    if carry is None:
        total = jnp.broadcast_to(total, (z.shape[0], LANES))
    else:
        arg = arg + jnp.tile(carry, (1, z.shape[1] // LANES))
        total = carry + total
    a = jnp.exp2(arg)
    if mask is not None:
        a = jnp.where(mask, a, 0.0)
    return a.astype(BF16), total


def _sb_prompt_kernel(q_ref, k_ref, v_ref, tri_ref, o_ref, carry_scr, acc_scr, *, tq):
    i = pl.program_id(1)
    q = q_ref[...]
    lane = lax.broadcasted_iota(jnp.int32, (tq, LANES), 1)
    zero = jnp.zeros((tq, LANES), BF16)
    groups = HEADS // 2
    qg = []
    for g in range(groups):
        pair = q[:, LANES * g:LANES * (g + 1)]
        qg.append(jnp.concatenate([jnp.where(lane < SB_DIM, pair, zero), jnp.where(lane >= SB_DIM, pair, zero)],
                                  axis=0))
    carry_scr[...] = jnp.zeros(carry_scr.shape, F32)
    acc_scr[...] = jnp.zeros(acc_scr.shape, F32)
    tri = tri_ref[...]

    def block(j, mask):
        start = pl.multiple_of(j * tq, tq)
        z = jnp.concatenate([_dot_nt(qg[g], k_ref[pl.ds(start, tq), LANES * g:LANES * (g + 1)])
                             for g in range(groups)], axis=0)
        a, carry = _sb_weights(z, carry_scr[...], tri, mask)
        carry_scr[...] = carry
        for g in range(groups):
            acc_scr[g] += _dot(a[2 * tq * g:2 * tq * (g + 1)], v_ref[pl.ds(start, tq), LANES * g:LANES * (g + 1)])

    t = lax.broadcasted_iota(jnp.int32, (HEADS * tq, tq), 0) & (tq - 1)
    col = lax.broadcasted_iota(jnp.int32, (HEADS * tq, tq), 1)
    block(i, col < t)

    def alive():
        return (jnp.max(carry_scr[...]) >= SB_DEAD).astype(jnp.int32)

    def cond(state):
        j, live = state
        return jnp.logical_and(j >= 0, live == 1)

    def body(state):
        j, _ = state
        block(j, None)
        return j - 1, alive()

    lax.while_loop(cond, body, (i - 1, alive()))

    for g in range(groups):
        o_ref[:, LANES * g:LANES * (g + 1)] = jnp.where(lane < SB_DIM, acc_scr[g, :tq], acc_scr[g, tq:])


def _sb_prompt(sbq, sbk, sbv, tri, batch, seq, tq):
    nq = seq // tq
    kv_spec = pl.BlockSpec((seq, WIDTH), lambda b, i: (b, 0), pipeline_mode=pl.Buffered(1))
    return pl.pallas_call(
        functools.partial(_sb_prompt_kernel, tq=tq),
        grid=(batch, nq),
        in_specs=[pl.BlockSpec((tq, WIDTH), lambda b, i: (b * nq + i, 0)), kv_spec, kv_spec,
                  _const_spec(tri.shape)],
        out_specs=pl.BlockSpec((tq, WIDTH), lambda b, i: (b * nq + i, 0)),
        out_shape=jax.ShapeDtypeStruct((batch * seq, WIDTH), F32),
        scratch_shapes=[pltpu.VMEM((HEADS * tq, LANES), F32), pltpu.VMEM((HEADS // 2, 2 * tq, LANES), F32)],
        compiler_params=pltpu.CompilerParams(dimension_semantics=("arbitrary", "arbitrary"),
                                             vmem_limit_bytes=VMEM_LIMIT),
        name="sb_prompt",
    )(sbq, sbk, sbv, tri)


SB_PREFETCH = 2
SB_AHEAD = SB_PREFETCH * PAGE
DEC_AHEAD = 2
DEC_SLOTS = DEC_AHEAD + 1


def _decode_kernel(pt_ref, q_ref, kvnew_ref, sbq_ref, sbknew_ref, sbvnew_ref, tri_ref, tripage_ref,
                   cmla_hbm, csbk_hbm, csbv_hbm, olat_ref, osb_ref,
                   kbuf, sbkbuf, sbvbuf, slowk, slowv, carry_scr, acc_scr,
                   sem_mla, sem_sbk, sem_sbv, sem_slow, *, n_seq, n_pages, dec):
    n = pl.program_id(0)
    slot = lax.rem(n, DEC_SLOTS)
    rows = HEADS * dec

    def mla_copy(seq, s, p):
        return pltpu.make_async_copy(cmla_hbm.at[pt_ref[seq, p]], kbuf.at[s, p], sem_mla.at[s])

    def sb_copy(seq, s, r, src, dst, sem):
        return pltpu.make_async_copy(src.at[pt_ref[seq, n_pages - 1 - r]], dst.at[s, SB_PREFETCH - 1 - r], sem.at[s])

    def fetch(seq, s):
        for p in range(n_pages):
            mla_copy(seq, s, p).start()
        for r in range(SB_PREFETCH):
            sb_copy(seq, s, r, csbk_hbm, sbkbuf, sem_sbk).start()
            sb_copy(seq, s, r, csbv_hbm, sbvbuf, sem_sbv).start()

    @pl.when(n == 0)
    def _():
        for a in range(min(DEC_AHEAD, n_seq)):
            fetch(a, a)

    @pl.when(n + DEC_AHEAD < n_seq)
    def _():
        fetch(n + DEC_AHEAD, lax.rem(n + DEC_AHEAD, DEC_SLOTS))

    lane = lax.broadcasted_iota(jnp.int32, (rows, WIDTH), 1)
    rowi = lax.broadcasted_iota(jnp.int32, (rows, WIDTH), 0)
    own = (lane >> (SB_DIM.bit_length() - 1)) == (rowi >> (dec.bit_length() - 1))
    qbd = jnp.where(own, jnp.tile(sbq_ref[0].astype(F32), (HEADS, 1)), 0.0).astype(BF16)
    tri = tri_ref[...]
    pad = jnp.zeros((PAGE - dec, WIDTH), F32)
    knew = jnp.concatenate([sbknew_ref[0], pad], axis=0).astype(BF16)
    vnew = jnp.concatenate([sbvnew_ref[0], pad], axis=0).astype(BF16)

    for r in range(SB_PREFETCH):
        sb_copy(n, slot, r, csbk_hbm, sbkbuf, sem_sbk).wait()
        sb_copy(n, slot, r, csbv_hbm, sbvbuf, sem_sbv).wait()

    for p in range(n_pages):
        pltpu.make_async_copy(cmla_hbm.at[0], kbuf.at[slot, p], sem_mla.at[slot]).wait()

    qf = q_ref[0].astype(F32)
    qs = jnp.concatenate([qf[:, QPAD * h:QPAD * h + MLA_ROW] for h in range(HEADS)], axis=0).astype(BF16)
    kc = jnp.concatenate([kbuf[slot, p].astype(BF16) for p in range(n_pages)], axis=1)
    s = _dot(qs, kc)
    kn = jnp.concatenate([kvnew_ref[0], jnp.zeros((PAGE - dec, MLA_ROW), F32)], axis=0).astype(BF16)
    t_mla = lax.broadcasted_iota(jnp.int32, (rows, PAGE), 0) & (dec - 1)
    col = lax.broadcasted_iota(jnp.int32, (rows, PAGE), 1)
    s_new = jnp.where(col <= t_mla, _dot_nt(qs, kn), -jnp.inf)
    m = jnp.maximum(jnp.max(s, axis=-1, keepdims=True), jnp.max(s_new, axis=-1, keepdims=True))
    p = jnp.exp2(s - m)
    p_new = jnp.exp2(s_new - m)
    l = jnp.sum(p, axis=-1, keepdims=True) + jnp.sum(p_new, axis=-1, keepdims=True)
    o = _dot_nt(p.astype(BF16), kc[:KV_RANK]) + _dot(p_new.astype(BF16), kn[:, :KV_RANK])
    olat_ref[0] = o * (1.0 / l)

    kt = jnp.concatenate([sbkbuf[slot, r].astype(BF16) for r in range(SB_PREFETCH)], axis=1)
    vt = jnp.concatenate([sbvbuf[slot, r].astype(BF16) for r in range(SB_PREFETCH)], axis=1)
    z = jnp.concatenate([_dot(qbd, kt), _dot_nt(qbd, knew)], axis=1)
    t_sb = lax.broadcasted_iota(jnp.int32, z.shape, 0) & (dec - 1)
    col_sb = lax.broadcasted_iota(jnp.int32, z.shape, 1)
    a, carry0 = _sb_weights(z, None, tri, col_sb < SB_AHEAD + t_sb)
    acc0 = _dot_nt(a[:, :SB_AHEAD], vt) + _dot(a[:, SB_AHEAD:], vnew)

    def own_heads(acc):
        sel = jnp.where(own, acc, 0.0)
        out = sel[:dec]
        for h in range(1, HEADS):
            out += sel[h * dec:(h + 1) * dec]
        return out

    osb_ref[0] = own_heads(acc0)

    @pl.when(jnp.max(carry0) >= SB_DEAD)
    def _():
        acc_scr[...] = acc0
        carry_scr[...] = carry0

        def cond(state):
            p, live = state
            return jnp.logical_and(p >= 0, live == 1)

        def body(state):
            p, _ = state
            ck = pltpu.make_async_copy(csbk_hbm.at[pt_ref[n, p]], slowk, sem_slow.at[0])
            cv = pltpu.make_async_copy(csbv_hbm.at[pt_ref[n, p]], slowv, sem_slow.at[1])
            ck.start()
            cv.start()
            ck.wait()
            cv.wait()
            a, c = _sb_weights(_dot(qbd, slowk[...].astype(BF16)), carry_scr[...], tripage_ref[...], None)
            acc_scr[...] += _dot_nt(a, slowv[...].astype(BF16))
            carry_scr[...] = c
            return p - 1, (jnp.max(c) >= SB_DEAD).astype(jnp.int32)

        lax.while_loop(cond, body, (jnp.int32(n_pages - 1 - SB_PREFETCH), jnp.int32(1)))
        osb_ref[0] = own_heads(acc_scr[...])


def _decode(page_table, qmla, kvnew, sbq, sbknew, sbvnew, tri, tripage, cache_mla_t, cache_sbk_t, cache_sbv_t):
    n_seq, n_pages = page_table.shape
    dec = kvnew.shape[1]
    rows = HEADS * dec
    per_seq = lambda r, w: pl.BlockSpec((1, r, w), lambda n, pt: (n, 0, 0))
    const = lambda shape: pl.BlockSpec(shape, lambda n, pt: (0,) * len(shape), pipeline_mode=pl.Buffered(1))
    hbm = pl.BlockSpec(memory_space=pl.ANY)
    grid_spec = pltpu.PrefetchScalarGridSpec(
        num_scalar_prefetch=1,
        grid=(n_seq,),
        in_specs=[per_seq(dec, HEADS * QPAD), per_seq(dec, MLA_ROW), per_seq(dec, WIDTH), per_seq(dec, WIDTH),
                  per_seq(dec, WIDTH), const(tri.shape), const(tripage.shape), hbm, hbm, hbm],
        out_specs=[per_seq(rows, KV_RANK), per_seq(dec, WIDTH)],
        scratch_shapes=[
            pltpu.VMEM((DEC_SLOTS, n_pages, MLA_ROW, PAGE), F32),
            pltpu.VMEM((DEC_SLOTS, SB_PREFETCH, WIDTH, PAGE), F32),
            pltpu.VMEM((DEC_SLOTS, SB_PREFETCH, WIDTH, PAGE), F32),
            pltpu.VMEM((WIDTH, PAGE), F32),
            pltpu.VMEM((WIDTH, PAGE), F32),
            pltpu.VMEM((rows, LANES), F32),
            pltpu.VMEM((rows, WIDTH), F32),
            pltpu.SemaphoreType.DMA((DEC_SLOTS,)),
            pltpu.SemaphoreType.DMA((DEC_SLOTS,)),
            pltpu.SemaphoreType.DMA((DEC_SLOTS,)),
            pltpu.SemaphoreType.DMA((2,)),
        ],
    )
    return pl.pallas_call(
        functools.partial(_decode_kernel, n_seq=n_seq, n_pages=n_pages, dec=dec),
        grid_spec=grid_spec,
        out_shape=[jax.ShapeDtypeStruct((n_seq, rows, KV_RANK), F32), jax.ShapeDtypeStruct((n_seq, dec, WIDTH), F32)],
        compiler_params=pltpu.CompilerParams(dimension_semantics=("arbitrary",), vmem_limit_bytes=VMEM_LIMIT),
        name="decode",
    )(page_table, qmla, kvnew, sbq, sbknew, sbvnew, tri, tripage, cache_mla_t, cache_sbk_t, cache_sbv_t)


def _uv_kernel(olat_ref, wuv_ref, o_ref, *, dec):
    o = olat_ref[...]
    seqs = o.shape[0]
    out = None
    for h in range(HEADS):
        oh = o[:, h * dec:(h + 1) * dec, :].reshape(seqs * dec, KV_RANK).astype(BF16)
        part = _dot(oh, wuv_ref[h])
        out = part if out is None else out + part
    o_ref[...] = out


def _uv(olat, wuv, tm):
    n_seq, rows, _ = olat.shape
    dec = rows // HEADS
    n = n_seq * dec
    return pl.pallas_call(
        functools.partial(_uv_kernel, dec=dec),
        grid=(n // tm,),
        in_specs=[pl.BlockSpec((tm // dec, rows, KV_RANK), lambda i: (i, 0, 0)), _const_spec(wuv.shape)],
        out_specs=pl.BlockSpec((tm, WIDTH), lambda i: (i, 0)),
        out_shape=jax.ShapeDtypeStruct((n, WIDTH), F32),
        compiler_params=pltpu.CompilerParams(dimension_semantics=("arbitrary",), vmem_limit_bytes=VMEM_LIMIT),
        name="uv",
    )(olat, wuv)


def _finish_kernel(x_ref, omla_ref, osb_ref, gmla_ref, gsb_ref, mmla_ref, msb_ref,
                   womla_ref, wosb_ref, wout_ref, y_ref):
    a = (omla_ref[...] * gmla_ref[...]).astype(BF16)
    b = (osb_ref[...] * gsb_ref[...]).astype(BF16)
    merged = mmla_ref[...] * _dot(a, womla_ref[...]) + msb_ref[...] * _dot(b, wosb_ref[...])
    y_ref[...] = x_ref[...] + _dot(merged.astype(BF16), wout_ref[...])


def _finish(x, omla, osb, gmla, gsb, mmla, msb, womla, wosb, wout, tm):
    n = x.shape[0]
    row = lambda w: pl.BlockSpec((tm, w), lambda i: (i, 0))
    return pl.pallas_call(
        _finish_kernel,
        grid=(n // tm,),
        in_specs=[row(D_MODEL), row(WIDTH), row(WIDTH), row(WIDTH), row(WIDTH), row(D_MODEL), row(D_MODEL),
                  _const_spec(womla.shape), _const_spec(wosb.shape), _const_spec(wout.shape)],
        out_specs=row(D_MODEL),
        out_shape=jax.ShapeDtypeStruct((n, D_MODEL), F32),
        compiler_params=pltpu.CompilerParams(dimension_semantics=("arbitrary",), vmem_limit_bytes=VMEM_LIMIT),
        name="finish",
    )(x, omla, osb, gmla, gsb, mmla, msb, womla, wosb, wout)


def _rope_tables(start, count):
    inv_freq = ROPE_BASE ** (-np.arange(HALF_ROPE, dtype=np.float64) / HALF_ROPE)
    ang = np.arange(start, start + count, dtype=np.float64)[:, None] * inv_freq[None, :]
    cos, sin = jnp.asarray(np.cos(ang), F32), jnp.asarray(np.sin(ang), F32)
    zeros = jnp.zeros((count, LANES - QK_ROPE), F32)
    return (jnp.tile(cos, (1, HEADS)), jnp.tile(sin, (1, HEADS)),
            jnp.concatenate([cos, cos, zeros], axis=1), jnp.concatenate([-sin, sin, zeros], axis=1))


def _rope_perm():
    p = np.zeros((2 * LANES, HEADS * LANES), np.float32)
    for h in range(HEADS):
        for d in range(HALF_ROPE):
            p[h * HALF_ROPE + d, h * LANES + d] = 1.0
            p[LANES + h * HALF_ROPE + d, h * LANES + HALF_ROPE + d] = 1.0
    return jnp.asarray(p, BF16)


def _pack_weights(norm_w, w_in, mla_q_a_norm_w, mla_w_uq, mla_q_head_norm_w, mla_w_uk,
                  mla_kv_a_norm_w, mla_k_rope_norm_w):
    kr0 = Q_RANK + KV_RANK
    wa = w_in[:, :kr0].astype(BF16)
    wb = w_in[:, kr0 + QK_ROPE:].astype(BF16)
    wc = jnp.pad(w_in[:, kr0:kr0 + QK_ROPE], ((0, 0), (0, LANES - QK_ROPE))).astype(BF16)
    assert wa.shape[1] == C_SBQ - C_QA and wb.shape[1] == C_KR - C_SBQ
    wuq = mla_w_uq.reshape(Q_RANK, HEADS, Q_HEAD)
    wuq = jnp.concatenate([wuq[:, :, :QK_NOPE].reshape(Q_RANK, -1),
                           wuq[:, :, QK_NOPE:QK_NOPE + HALF_ROPE].reshape(Q_RANK, -1),
                           wuq[:, :, QK_NOPE + HALF_ROPE:].reshape(Q_RANK, -1)], axis=1).astype(BF16)
    g = mla_q_head_norm_w
    gq = jnp.concatenate([jnp.tile(g[:QK_NOPE], HEADS), jnp.tile(g[QK_NOPE:QK_NOPE + HALF_ROPE], HEADS),
                          jnp.tile(g[QK_NOPE + HALF_ROPE:], HEADS)])[None, :]
    wuk_t = jnp.transpose(mla_w_uk, (1, 2, 0))
    zeros = jnp.zeros_like(wuk_t)
    odd = (jnp.arange(HEADS) % 2 == 1)[:, None, None]
    wuk = jnp.concatenate([jnp.where(odd, zeros, wuk_t), jnp.where(odd, wuk_t, zeros)], axis=1).astype(BF16)
    krnw = jnp.concatenate([mla_k_rope_norm_w, jnp.zeros((LANES - QK_ROPE,), F32)])[None, :]
    return [norm_w[None, :], wa, wb, wc, mla_q_a_norm_w[None, :], wuq, gq, wuk, _rope_perm(),
            mla_kv_a_norm_w[None, :], krnw]


def _pack_wuv(mla_w_uv):
    tiled = jnp.tile(jnp.transpose(mla_w_uv, (1, 0, 2)), (1, 1, HEADS))
    own = (jnp.arange(WIDTH) // V_HEAD)[None, None, :] == jnp.arange(HEADS)[:, None, None]
    return jnp.where(own, tiled, 0.0).astype(BF16)


def _tile(n, cap):
    t = cap
    while n % t:
        t //= 2
    return t


def kernel(x_prompt, x_sample, cache_mla_kv, cache_sb_k, cache_sb_v, page_table, norm_w, w_in,
           mla_q_a_norm_w, mla_w_uq, mla_q_head_norm_w, mla_w_uk, mla_kv_a_norm_w, mla_k_rope_norm_w,
           mla_w_uv, w_o_mla, w_o_sb, w_out):
    batch, seq, _ = x_prompt.shape
    n_seq, dec, _ = x_sample.shape
    n_pages = page_table.shape[1]
    past = n_pages * PAGE
    assert n_pages > SB_PREFETCH and dec == 8 and HEADS == 8 and seq % 512 == 0

    weights = _pack_weights(norm_w, w_in, mla_q_a_norm_w, mla_w_uq, mla_q_head_norm_w, mla_w_uk,
                            mla_kv_a_norm_w, mla_k_rope_norm_w)
    wuv = _pack_wuv(mla_w_uv)
    womla, wosb, wout = w_o_mla.astype(BF16), w_o_sb.astype(BF16), w_out.astype(BF16)

    tq = _tile(seq, 256)
    tm = _tile(seq, 256)
    xp = x_prompt.reshape(batch * seq, D_MODEL)
    tabs = _rope_tables(0, seq)
    ntile = seq // tm
    (mlakv_p, sbk_p, sbv_p, qmla, kvbf, sbq, sbkbf, sbvbf, gmla, gsb, mmla, msb) = _proj(
        xp, tabs, lambda i: (i % ntile, 0), weights, tm, groups=batch)
    tri = _tri2(tq)
    omla = _mla_prompt(qmla, kvbf, wuv, batch, seq, tq)
    osb = _sb_prompt(sbq, sbkbf, sbvbf, tri, batch, seq, tq)
    y_prompt = _finish(xp, omla, osb, gmla, gsb, mmla, msb, womla, wosb, wout, _tile(seq, 512))

    ns = n_seq * dec
    tms = _tile(ns, 256)
    xs = x_sample.reshape(ns, D_MODEL)
    tabs_s = [jnp.tile(t, (tms // dec, 1)) for t in _rope_tables(past, dec)]
    (mlakv_s, sbk_s, sbv_s, qmla_s, _, sbq_s, _, _, gmla_s, gsb_s, mmla_s, msb_s) = _proj(
        xs, tabs_s, lambda i: (0, 0), weights, tms)
    tri_s = _tri2(SB_AHEAD + PAGE)
    per_seq = lambda a: a.reshape(n_seq, dec, a.shape[-1])
    pool = cache_mla_kv.shape[0]
    cache_mla_t = jnp.transpose(cache_mla_kv, (0, 2, 1))
    cache_sbk_t = jnp.transpose(cache_sb_k, (0, 2, 3, 1)).reshape(pool, WIDTH, PAGE)
    cache_sbv_t = jnp.transpose(cache_sb_v, (0, 2, 3, 1)).reshape(pool, WIDTH, PAGE)
    olat_s, osb_s = _decode(page_table, per_seq(qmla_s), per_seq(mlakv_s), per_seq(sbq_s),
                            per_seq(sbk_s), per_seq(sbv_s), tri_s, _tri2(PAGE), cache_mla_t, cache_sbk_t, cache_sbv_t)
    omla_s = _uv(olat_s, wuv, tms)
    y_sample = _finish(xs, omla_s, osb_s.reshape(ns, WIDTH), gmla_s, gsb_s, mmla_s, msb_s,
                       womla, wosb, wout, tms)

    heads = lambda a, b, t: a.reshape(b, t, HEADS, SB_DIM)
    heads_t = lambda a: jnp.transpose(a.reshape(batch, HEADS, SB_DIM, seq), (0, 3, 1, 2))
    return (y_prompt.reshape(batch, seq, D_MODEL), y_sample.reshape(n_seq, dec, D_MODEL),
            jnp.transpose(mlakv_p, (0, 2, 1)), heads_t(sbk_p), heads_t(sbv_p),
            mlakv_s.reshape(n_seq, dec, MLA_ROW), heads(sbk_s, n_seq, dec), heads(sbv_s, n_seq, dec))
```

```python
import functools

import numpy as np
import jax
import jax.numpy as jnp
from jax import lax
from jax.experimental import pallas as pl
from jax.experimental.pallas import tpu as pltpu

F32 = jnp.float32
BF16 = jnp.bfloat16

D_MODEL = 1024
HEADS = 8
QK_NOPE = 64
QK_ROPE = 32
HALF_ROPE = QK_ROPE // 2
Q_HEAD = QK_NOPE + QK_ROPE
V_HEAD = 64
KV_RANK = 256
Q_RANK = 768
SB_DIM = 64
WIDTH = HEADS * V_HEAD
MLA_ROW = KV_RANK + QK_ROPE
PAGE = 128
ROPE_BASE = 10000.0
RMS_EPS = 1e-6
MLA_SCALE = Q_HEAD ** -0.5
SB_SCALE = SB_DIM ** -0.5

LANES = 128
QPAD = 3 * LANES
LOG2E = 1.4426950408889634
Q_SCALE = MLA_SCALE * LOG2E
SBQ_SCALE = SB_SCALE * LOG2E
SB_DEAD = -151.0
VMEM_LIMIT = 56 * 1024 * 1024

C_QA = 0
C_CKV = C_QA + Q_RANK
C_SBQ = C_CKV + KV_RANK
C_SBK = C_SBQ + WIDTH
C_SBV = C_SBK + WIDTH
C_GMLA = C_SBV + WIDTH
C_GSB = C_GMLA + WIDTH
C_MMLA = C_GSB + WIDTH
C_MSB = C_MMLA + D_MODEL
C_KR = C_MSB + D_MODEL
C_END = C_KR + LANES


def _rms(x, n):
    return x * lax.rsqrt(jnp.sum(x * x, axis=-1, keepdims=True) * (1.0 / n) + RMS_EPS)


def _sigmoid(x):
    return 1.0 / (1.0 + jnp.exp(-x))


def _split_bf16(x):
    hi = x.astype(BF16)
    lo = (x - hi.astype(F32)).astype(BF16)
    return hi, lo


def _dot(a, b):
    return jnp.dot(a, b, preferred_element_type=F32)


def _dot_nt(a, b):
    return lax.dot_general(a, b, (((1,), (1,)), ((), ())), preferred_element_type=F32)


def _proj_kernel(x_ref, cq_ref, sq_ref, ck_ref, sk_ref, normw_ref, wa_ref, wb_ref, wc_ref, qanw_ref,
                 wuq_ref, gq_ref, wuk_ref, perm_ref, kvnw_ref, krnw_ref,
                 mlakv_ref, sbk_ref, sbv_ref, qmla_ref, kvbf_ref, sbq_ref, sbkbf_ref,
                 sbvbf_ref, gmla_ref, gsb_ref, mmla_ref, msb_ref, *, transposed):
    x = x_ref[...]
    xn = (_rms(x, D_MODEL) * normw_ref[...]).astype(BF16)

    def proj(a, b):
        for ref, lo, hi in ((wa_ref, C_QA, C_SBQ), (wb_ref, C_SBQ, C_KR), (wc_ref, C_KR, C_END)):
            if lo <= a and b <= hi:
                return _dot(xn, ref[:, a - lo:b - lo])
        raise ValueError((a, b))

    q_a = proj(C_QA, C_CKV)
    qan = (_rms(q_a, Q_RANK) * qanw_ref[...]).astype(BF16)
    q = _dot(qan, wuq_ref[...])
    q2 = q * q
    nope_w = HEADS * QK_NOPE
    rope2 = q2[:, nope_w:nope_w + LANES] + q2[:, nope_w + LANES:]
    lane = lax.broadcasted_iota(jnp.int32, rope2.shape, 1)
    upper = lane >= QK_NOPE
    rope_head = lane >> (HALF_ROPE.bit_length() - 1)
    inv = []
    for h in range(HEADS):
        grp = q2[:, LANES * (h // 2):LANES * (h // 2 + 1)]
        mine = jnp.where(upper if h % 2 else ~upper, grp, 0.0) + jnp.where(rope_head == h, rope2, 0.0)
        inv.append(lax.rsqrt(jnp.sum(mine, axis=-1, keepdims=True) * (1.0 / Q_HEAD) + RMS_EPS))
    rope_inv = inv[0]
    for h in range(1, HEADS):
        rope_inv = jnp.where(rope_head == h, inv[h], rope_inv)
    scale = jnp.concatenate([jnp.where(upper, inv[2 * g + 1], inv[2 * g]) for g in range(HEADS // 2)]
                            + [rope_inv, rope_inv], axis=1)
    qn = q * scale * gq_ref[...]
    x1 = qn[:, nope_w:nope_w + LANES]
    x2 = qn[:, nope_w + LANES:]
    cq = cq_ref[...]
    sq = sq_ref[...]
    rot = (jnp.concatenate([x1 * cq - x2 * sq, x2 * cq + x1 * sq], axis=1) * Q_SCALE).astype(BF16)
    rope = _dot(rot, perm_ref[...]).astype(BF16)
    for h in range(HEADS):
        pair = qn[:, LANES * (h // 2):LANES * (h // 2 + 1)].astype(BF16)
        q_lat = _dot(pair, wuk_ref[h]) * Q_SCALE
        qmla_ref[:, QPAD * h:QPAD * h + KV_RANK] = q_lat.astype(BF16)
        qmla_ref[:, QPAD * h + KV_RANK:QPAD * (h + 1)] = rope[:, LANES * h:LANES * (h + 1)]

    cn = _rms(proj(C_CKV, C_SBQ), KV_RANK) * kvnw_ref[...]
    k_r = proj(C_KR, C_END)
    krn = _rms(k_r, QK_ROPE) * krnw_ref[...]
    lane = lax.broadcasted_iota(jnp.int32, krn.shape, 1)
    swapped = jnp.where(lane < HALF_ROPE, pltpu.roll(krn, LANES - HALF_ROPE, 1), pltpu.roll(krn, HALF_ROPE, 1))
    kro = krn * ck_ref[...] + swapped * sk_ref[...]
    if transposed:
        mlakv_ref[0, :KV_RANK, :] = cn.T
        mlakv_ref[0, KV_RANK:, :] = kro.T[:QK_ROPE]
    else:
        mlakv_ref[:, :KV_RANK] = cn
        mlakv_ref[:, KV_RANK:] = kro[:, :QK_ROPE]
    kvbf_ref[:, :KV_RANK] = cn.astype(BF16)
    kvbf_ref[:, KV_RANK:] = kro.astype(BF16)

    sbq_ref[...] = (proj(C_SBQ, C_SBK) * SBQ_SCALE).astype(BF16)
    for c0, f32_ref, bf_ref in ((C_SBK, sbk_ref, sbkbf_ref), (C_SBV, sbv_ref, sbvbf_ref)):
        val = proj(c0, c0 + WIDTH)
        if transposed:
            f32_ref[0] = val.T
        else:
            f32_ref[...] = val
        bf_ref[...] = val.astype(BF16)

    g = proj(C_GMLA, C_GSB)
    gmla_ref[...] = (g * _sigmoid(g)).astype(BF16)
    g = proj(C_GSB, C_MMLA)
    gsb_ref[...] = (g * _sigmoid(g)).astype(BF16)
    mmla_ref[...] = _sigmoid(proj(C_MMLA, C_MSB)).astype(BF16)
    msb_ref[...] = _sigmoid(proj(C_MSB, C_KR)).astype(BF16)


def _const_spec(shape):
    nd = len(shape)
    return pl.BlockSpec(shape, lambda *_: (0,) * nd, pipeline_mode=pl.Buffered(1))


def _proj(x, tables, table_map, weights, tm, groups=None):
    n = x.shape[0]
    row = lambda w: pl.BlockSpec((tm, w), lambda i: (i, 0))
    tab = pl.BlockSpec((tm, LANES), table_map)
    widths = [(MLA_ROW, F32), (WIDTH, F32), (WIDTH, F32), (HEADS * QPAD, BF16), (QPAD, BF16),
              (WIDTH, BF16), (WIDTH, BF16), (WIDTH, BF16), (WIDTH, BF16), (WIDTH, BF16),
              (D_MODEL, BF16), (D_MODEL, BF16)]
    out_specs = [row(w) for w, _ in widths]
    out_shape = [jax.ShapeDtypeStruct((n, w), dt) for w, dt in widths]
    if groups is not None:
        per = n // groups
        ntile = per // tm
        for o in range(3):
            w = widths[o][0]
            out_specs[o] = pl.BlockSpec((1, w, tm), lambda i: (i // ntile, 0, i % ntile))
            out_shape[o] = jax.ShapeDtypeStruct((groups, w, per), F32)
    return pl.pallas_call(
        functools.partial(_proj_kernel, transposed=groups is not None),
        grid=(n // tm,),
        in_specs=[row(D_MODEL), tab, tab, tab, tab] + [_const_spec(w.shape) for w in weights],
        out_specs=out_specs,
        out_shape=out_shape,
        compiler_params=pltpu.CompilerParams(dimension_semantics=("arbitrary",), vmem_limit_bytes=VMEM_LIMIT),
        name="proj",
    )(x, *tables, *weights)


def _mla_prompt_kernel(q_ref, kv_ref, wuv_ref, o_ref, s_scr, m_scr, l_scr, acc_scr, *, tq):
    i = pl.program_id(1)
    tk = 2 * tq
    q = q_ref[...]
    qs = jnp.concatenate([q[:, QPAD * h:QPAD * (h + 1)] for h in range(HEADS)], axis=0)
    m_scr[...] = jnp.full(m_scr.shape, -jnp.inf, F32)
    l_scr[...] = jnp.zeros(l_scr.shape, F32)
    acc_scr[...] = jnp.zeros(acc_scr.shape, F32)
    last = i >> 1

    def keys(j):
        return kv_ref[pl.ds(pl.multiple_of(j * tk, tk), tk), :]

    def scores(buf, j):
        s_scr[buf] = _dot_nt(qs, keys(j))

    def update(buf, j, diagonal, half=False):
        s = s_scr[buf]
        w = tq if half else tk
        if half:
            s = s[:, :tq]
        if diagonal:
            t = lax.broadcasted_iota(jnp.int32, s.shape, 0) & (tq - 1)
            col = lax.broadcasted_iota(jnp.int32, s.shape, 1)
            s = jnp.where(col <= t + (0 if half else tq), s, -jnp.inf)
        m_prev = m_scr[...]
        m_new = jnp.maximum(m_prev, jnp.max(s, axis=-1, keepdims=True))
        alpha = jnp.exp2(m_prev - m_new)
        p = jnp.exp2(s - jnp.tile(m_new, (1, w // LANES)))
        l_scr[...] = alpha * l_scr[...] + jnp.sum(p, axis=-1, keepdims=True)
        acc_scr[...] = (jnp.tile(alpha, (1, KV_RANK // LANES)) * acc_scr[...]
                        + _dot(p.astype(BF16), keys(j)[:w, :KV_RANK]))
        m_scr[...] = m_new

    scores(0, 0)

    def body(jj, carry):
        j = 2 * jj
        scores(1, j + 1)
        update(0, j, False)
        scores(0, j + 2)
        update(1, j + 1, False)
        return carry

    lax.fori_loop(0, last >> 1, body, 0)

    for even_tile in (True, False):
        par = 0 if even_tile else 1

        @pl.when(jnp.logical_and((last & 1) == 0, (i & 1) == par))
        def _(even_tile=even_tile):
            update(0, last, True, half=even_tile)

        @pl.when(jnp.logical_and((last & 1) == 1, (i & 1) == par))
        def _(even_tile=even_tile):
            scores(1, last)
            update(0, last - 1, False)
            update(1, last, True, half=even_tile)

    o_lat = (acc_scr[...] * jnp.tile(1.0 / l_scr[...], (1, KV_RANK // LANES))).astype(BF16)
    out = _dot(o_lat[:tq], wuv_ref[0])
    for h in range(1, HEADS):
        out += _dot(o_lat[h * tq:(h + 1) * tq], wuv_ref[h])
    o_ref[...] = out


def _mla_prompt(qmla, kvbf, wuv, batch, seq, tq):
    nq = seq // tq
    rows = HEADS * tq
    return pl.pallas_call(
        functools.partial(_mla_prompt_kernel, tq=tq),
        grid=(batch, nq),
        in_specs=[pl.BlockSpec((tq, HEADS * QPAD), lambda b, i: (b * nq + i, 0)),
                  pl.BlockSpec((seq, QPAD), lambda b, i: (b, 0)),
                  _const_spec(wuv.shape)],
        out_specs=pl.BlockSpec((tq, WIDTH), lambda b, i: (b * nq + i, 0)),
        out_shape=jax.ShapeDtypeStruct((batch * seq, WIDTH), F32),
        scratch_shapes=[pltpu.VMEM((2, rows, 2 * tq), F32), pltpu.VMEM((rows, LANES), F32),
                        pltpu.VMEM((rows, LANES), F32), pltpu.VMEM((rows, KV_RANK), F32)],
        compiler_params=pltpu.CompilerParams(dimension_semantics=("arbitrary", "arbitrary"),
                                             vmem_limit_bytes=VMEM_LIMIT),
        name="mla_prompt",
    )(qmla, kvbf, wuv)


def _tri2(tk):
    tri = np.tril(np.ones((tk, tk), np.float32))
    return jnp.asarray(np.concatenate([tri, tri], axis=0), BF16)


def _sb_weights(z, carry, tri2, mask):
    nz = -z
    lk = jnp.minimum(nz, 0.0) - jnp.log2(1.0 + jnp.exp2(jnp.minimum(z, nz)))
    if mask is not None:
        lk = jnp.where(mask, lk, 0.0)
    hi, lo = _split_bf16(lk)
    arg = z + _dot(jnp.concatenate([hi, lo], axis=1), tri2)
    total = jnp.sum(lk, axis=-1, keepdims=True)
    if carry is None:
        total = jnp.broadcast_to(total, (z.shape[0], LANES))
    else:
        arg = arg + jnp.tile(carry, (1, z.shape[1] // LANES))
        total = carry + total
    a = jnp.exp2(arg)
    if mask is not None:
        a = jnp.where(mask, a, 0.0)
    return a.astype(BF16), total


def _sb_prompt_kernel(q_ref, k_ref, v_ref, tri_ref, o_ref, carry_scr, acc_scr, *, tq):
    i = pl.program_id(1)
    q = q_ref[...]
    lane = lax.broadcasted_iota(jnp.int32, (tq, LANES), 1)
    zero = jnp.zeros((tq, LANES), BF16)
    groups = HEADS // 2
    qg = []
    for g in range(groups):
        pair = q[:, LANES * g:LANES * (g + 1)]
        qg.append(jnp.concatenate([jnp.where(lane < SB_DIM, pair, zero), jnp.where(lane >= SB_DIM, pair, zero)],
                                  axis=0))
    carry_scr[...] = jnp.zeros(carry_scr.shape, F32)
    acc_scr[...] = jnp.zeros(acc_scr.shape, F32)
    tri = tri_ref[...]

    def block(j, mask):
        start = pl.multiple_of(j * tq, tq)
        z = jnp.concatenate([_dot_nt(qg[g], k_ref[pl.ds(start, tq), LANES * g:LANES * (g + 1)])
                             for g in range(groups)], axis=0)
        a, carry = _sb_weights(z, carry_scr[...], tri, mask)
        carry_scr[...] = carry
        for g in range(groups):
            acc_scr[g] += _dot(a[2 * tq * g:2 * tq * (g + 1)], v_ref[pl.ds(start, tq), LANES * g:LANES * (g + 1)])

    t = lax.broadcasted_iota(jnp.int32, (HEADS * tq, tq), 0) & (tq - 1)
    col = lax.broadcasted_iota(jnp.int32, (HEADS * tq, tq), 1)
    block(i, col < t)

    def alive():
        return (jnp.max(carry_scr[...]) >= SB_DEAD).astype(jnp.int32)

    def cond(state):
        j, live = state
        return jnp.logical_and(j >= 0, live == 1)

    def body(state):
        j, _ = state
        block(j, None)
        return j - 1, alive()

    lax.while_loop(cond, body, (i - 1, alive()))

    for g in range(groups):
        o_ref[:, LANES * g:LANES * (g + 1)] = jnp.where(lane < SB_DIM, acc_scr[g, :tq], acc_scr[g, tq:])


def _sb_prompt(sbq, sbk, sbv, tri, batch, seq, tq):
    nq = seq // tq
    kv_spec = pl.BlockSpec((seq, WIDTH), lambda b, i: (b, 0), pipeline_mode=pl.Buffered(1))
    return pl.pallas_call(
        functools.partial(_sb_prompt_kernel, tq=tq),
        grid=(batch, nq),
        in_specs=[pl.BlockSpec((tq, WIDTH), lambda b, i: (b * nq + i, 0)), kv_spec, kv_spec,
                  _const_spec(tri.shape)],
        out_specs=pl.BlockSpec((tq, WIDTH), lambda b, i: (b * nq + i, 0)),
        out_shape=jax.ShapeDtypeStruct((batch * seq, WIDTH), F32),
        scratch_shapes=[pltpu.VMEM((HEADS * tq, LANES), F32), pltpu.VMEM((HEADS // 2, 2 * tq, LANES), F32)],
        compiler_params=pltpu.CompilerParams(dimension_semantics=("arbitrary", "arbitrary"),
                                             vmem_limit_bytes=VMEM_LIMIT),
        name="sb_prompt",
    )(sbq, sbk, sbv, tri)


SB_PREFETCH = 2
SB_AHEAD = SB_PREFETCH * PAGE
DEC_AHEAD = 2
DEC_SLOTS = DEC_AHEAD + 1


def _decode_kernel(pt_ref, q_ref, kvnew_ref, sbq_ref, sbknew_ref, sbvnew_ref, tri_ref, tripage_ref,
                   cmla_hbm, csbk_hbm, csbv_hbm, olat_ref, osb_ref,
                   kbuf, sbkbuf, sbvbuf, slowk, slowv, carry_scr, acc_scr,
                   sem_mla, sem_sbk, sem_sbv, sem_slow, *, n_seq, n_pages, dec):
    n = pl.program_id(0)
    slot = lax.rem(n, DEC_SLOTS)
    rows = HEADS * dec

    def mla_copy(seq, s, p):
        return pltpu.make_async_copy(cmla_hbm.at[pt_ref[seq, p]], kbuf.at[s, p], sem_mla.at[s])

    def sb_copy(seq, s, r, src, dst, sem):
        return pltpu.make_async_copy(src.at[pt_ref[seq, n_pages - 1 - r]], dst.at[s, SB_PREFETCH - 1 - r], sem.at[s])

    def fetch(seq, s):
        for p in range(n_pages):
            mla_copy(seq, s, p).start()
        for r in range(SB_PREFETCH):
            sb_copy(seq, s, r, csbk_hbm, sbkbuf, sem_sbk).start()
            sb_copy(seq, s, r, csbv_hbm, sbvbuf, sem_sbv).start()

    @pl.when(n == 0)
    def _():
        for a in range(min(DEC_AHEAD, n_seq)):
            fetch(a, a)

    @pl.when(n + DEC_AHEAD < n_seq)
    def _():
        fetch(n + DEC_AHEAD, lax.rem(n + DEC_AHEAD, DEC_SLOTS))

    lane = lax.broadcasted_iota(jnp.int32, (rows, WIDTH), 1)
    rowi = lax.broadcasted_iota(jnp.int32, (rows, WIDTH), 0)
    own = (lane >> (SB_DIM.bit_length() - 1)) == (rowi >> (dec.bit_length() - 1))
    qbd = jnp.where(own, jnp.tile(sbq_ref[0].astype(F32), (HEADS, 1)), 0.0).astype(BF16)
    tri = tri_ref[...]
    pad = jnp.zeros((PAGE - dec, WIDTH), F32)
    knew = jnp.concatenate([sbknew_ref[0], pad], axis=0).astype(BF16)
    vnew = jnp.concatenate([sbvnew_ref[0], pad], axis=0).astype(BF16)

    for r in range(SB_PREFETCH):
        sb_copy(n, slot, r, csbk_hbm, sbkbuf, sem_sbk).wait()
        sb_copy(n, slot, r, csbv_hbm, sbvbuf, sem_sbv).wait()

    for p in range(n_pages):
        pltpu.make_async_copy(cmla_hbm.at[0], kbuf.at[slot, p], sem_mla.at[slot]).wait()

    qf = q_ref[0].astype(F32)
    qs = jnp.concatenate([qf[:, QPAD * h:QPAD * h + MLA_ROW] for h in range(HEADS)], axis=0).astype(BF16)
    kc = jnp.concatenate([kbuf[slot, p].astype(BF16) for p in range(n_pages)], axis=1)
    s = _dot(qs, kc)
    kn = jnp.concatenate([kvnew_ref[0], jnp.zeros((PAGE - dec, MLA_ROW), F32)], axis=0).astype(BF16)
    t_mla = lax.broadcasted_iota(jnp.int32, (rows, PAGE), 0) & (dec - 1)
    col = lax.broadcasted_iota(jnp.int32, (rows, PAGE), 1)
    s_new = jnp.where(col <= t_mla, _dot_nt(qs, kn), -jnp.inf)
    m = jnp.maximum(jnp.max(s, axis=-1, keepdims=True), jnp.max(s_new, axis=-1, keepdims=True))
    p = jnp.exp2(s - m)
    p_new = jnp.exp2(s_new - m)
    l = jnp.sum(p, axis=-1, keepdims=True) + jnp.sum(p_new, axis=-1, keepdims=True)
    o = _dot_nt(p.astype(BF16), kc[:KV_RANK]) + _dot(p_new.astype(BF16), kn[:, :KV_RANK])
    olat_ref[0] = o * (1.0 / l)

    kt = jnp.concatenate([sbkbuf[slot, r].astype(BF16) for r in range(SB_PREFETCH)], axis=1)
    vt = jnp.concatenate([sbvbuf[slot, r].astype(BF16) for r in range(SB_PREFETCH)], axis=1)
    z = jnp.concatenate([_dot(qbd, kt), _dot_nt(qbd, knew)], axis=1)
    t_sb = lax.broadcasted_iota(jnp.int32, z.shape, 0) & (dec - 1)
    col_sb = lax.broadcasted_iota(jnp.int32, z.shape, 1)
    a, carry0 = _sb_weights(z, None, tri, col_sb < SB_AHEAD + t_sb)
    acc0 = _dot_nt(a[:, :SB_AHEAD], vt) + _dot(a[:, SB_AHEAD:], vnew)

    def own_heads(acc):
        sel = jnp.where(own, acc, 0.0)
        out = sel[:dec]
        for h in range(1, HEADS):
            out += sel[h * dec:(h + 1) * dec]
        return out

    osb_ref[0] = own_heads(acc0)

    @pl.when(jnp.max(carry0) >= SB_DEAD)
    def _():
        acc_scr[...] = acc0
        carry_scr[...] = carry0

        def cond(state):
            p, live = state
            return jnp.logical_and(p >= 0, live == 1)

        def body(state):
            p, _ = state
            ck = pltpu.make_async_copy(csbk_hbm.at[pt_ref[n, p]], slowk, sem_slow.at[0])
            cv = pltpu.make_async_copy(csbv_hbm.at[pt_ref[n, p]], slowv, sem_slow.at[1])
            ck.start()
            cv.start()
            ck.wait()
            cv.wait()
            a, c = _sb_weights(_dot(qbd, slowk[...].astype(BF16)), carry_scr[...], tripage_ref[...], None)
            acc_scr[...] += _dot_nt(a, slowv[...].astype(BF16))
            carry_scr[...] = c
            return p - 1, (jnp.max(c) >= SB_DEAD).astype(jnp.int32)

        lax.while_loop(cond, body, (jnp.int32(n_pages - 1 - SB_PREFETCH), jnp.int32(1)))
        osb_ref[0] = own_heads(acc_scr[...])


def _decode(page_table, qmla, kvnew, sbq, sbknew, sbvnew, tri, tripage, cache_mla_t, cache_sbk_t, cache_sbv_t):
    n_seq, n_pages = page_table.shape
    dec = kvnew.shape[1]
    rows = HEADS * dec
    per_seq = lambda r, w: pl.BlockSpec((1, r, w), lambda n, pt: (n, 0, 0))
    const = lambda shape: pl.BlockSpec(shape, lambda n, pt: (0,) * len(shape), pipeline_mode=pl.Buffered(1))
    hbm = pl.BlockSpec(memory_space=pl.ANY)
    grid_spec = pltpu.PrefetchScalarGridSpec(
        num_scalar_prefetch=1,
        grid=(n_seq,),
        in_specs=[per_seq(dec, HEADS * QPAD), per_seq(dec, MLA_ROW), per_seq(dec, WIDTH), per_seq(dec, WIDTH),
                  per_seq(dec, WIDTH), const(tri.shape), const(tripage.shape), hbm, hbm, hbm],
        out_specs=[per_seq(rows, KV_RANK), per_seq(dec, WIDTH)],
        scratch_shapes=[
            pltpu.VMEM((DEC_SLOTS, n_pages, MLA_ROW, PAGE), F32),
            pltpu.VMEM((DEC_SLOTS, SB_PREFETCH, WIDTH, PAGE), F32),
            pltpu.VMEM((DEC_SLOTS, SB_PREFETCH, WIDTH, PAGE), F32),
            pltpu.VMEM((WIDTH, PAGE), F32),
            pltpu.VMEM((WIDTH, PAGE), F32),
            pltpu.VMEM((rows, LANES), F32),
            pltpu.VMEM((rows, WIDTH), F32),
            pltpu.SemaphoreType.DMA((DEC_SLOTS,)),
            pltpu.SemaphoreType.DMA((DEC_SLOTS,)),
            pltpu.SemaphoreType.DMA((DEC_SLOTS,)),
            pltpu.SemaphoreType.DMA((2,)),
        ],
    )
    return pl.pallas_call(
        functools.partial(_decode_kernel, n_seq=n_seq, n_pages=n_pages, dec=dec),
        grid_spec=grid_spec,
        out_shape=[jax.ShapeDtypeStruct((n_seq, rows, KV_RANK), F32), jax.ShapeDtypeStruct((n_seq, dec, WIDTH), F32)],
        compiler_params=pltpu.CompilerParams(dimension_semantics=("arbitrary",), vmem_limit_bytes=VMEM_LIMIT),
        name="decode",
    )(page_table, qmla, kvnew, sbq, sbknew, sbvnew, tri, tripage, cache_mla_t, cache_sbk_t, cache_sbv_t)


def _uv_kernel(olat_ref, wuv_ref, o_ref, *, dec):
    o = olat_ref[...]
    seqs = o.shape[0]
    out = None
    for h in range(HEADS):
        oh = o[:, h * dec:(h + 1) * dec, :].reshape(seqs * dec, KV_RANK).astype(BF16)
        part = _dot(oh, wuv_ref[h])
        out = part if out is None else out + part
    o_ref[...] = out


def _uv(olat, wuv, tm):
    n_seq, rows, _ = olat.shape
    dec = rows // HEADS
    n = n_seq * dec
    return pl.pallas_call(
        functools.partial(_uv_kernel, dec=dec),
        grid=(n // tm,),
        in_specs=[pl.BlockSpec((tm // dec, rows, KV_RANK), lambda i: (i, 0, 0)), _const_spec(wuv.shape)],
        out_specs=pl.BlockSpec((tm, WIDTH), lambda i: (i, 0)),
        out_shape=jax.ShapeDtypeStruct((n, WIDTH), F32),
        compiler_params=pltpu.CompilerParams(dimension_semantics=("arbitrary",), vmem_limit_bytes=VMEM_LIMIT),
        name="uv",
    )(olat, wuv)


def _finish_kernel(x_ref, omla_ref, osb_ref, gmla_ref, gsb_ref, mmla_ref, msb_ref,
                   womla_ref, wosb_ref, wout_ref, y_ref):
    a = (omla_ref[...] * gmla_ref[...]).astype(BF16)
    b = (osb_ref[...] * gsb_ref[...]).astype(BF16)
    merged = mmla_ref[...] * _dot(a, womla_ref[...]) + msb_ref[...] * _dot(b, wosb_ref[...])
    y_ref[...] = x_ref[...] + _dot(merged.astype(BF16), wout_ref[...])


def _finish(x, omla, osb, gmla, gsb, mmla, msb, womla, wosb, wout, tm):
    n = x.shape[0]
    row = lambda w: pl.BlockSpec((tm, w), lambda i: (i, 0))
    return pl.pallas_call(
        _finish_kernel,
        grid=(n // tm,),
        in_specs=[row(D_MODEL), row(WIDTH), row(WIDTH), row(WIDTH), row(WIDTH), row(D_MODEL), row(D_MODEL),
                  _const_spec(womla.shape), _const_spec(wosb.shape), _const_spec(wout.shape)],
        out_specs=row(D_MODEL),
        out_shape=jax.ShapeDtypeStruct((n, D_MODEL), F32),
        compiler_params=pltpu.CompilerParams(dimension_semantics=("arbitrary",), vmem_limit_bytes=VMEM_LIMIT),
        name="finish",
    )(x, omla, osb, gmla, gsb, mmla, msb, womla, wosb, wout)


def _rope_tables(start, count):
    inv_freq = ROPE_BASE ** (-np.arange(HALF_ROPE, dtype=np.float64) / HALF_ROPE)
    ang = np.arange(start, start + count, dtype=np.float64)[:, None] * inv_freq[None, :]
    cos, sin = jnp.asarray(np.cos(ang), F32), jnp.asarray(np.sin(ang), F32)
    zeros = jnp.zeros((count, LANES - QK_ROPE), F32)
    return (jnp.tile(cos, (1, HEADS)), jnp.tile(sin, (1, HEADS)),
            jnp.concatenate([cos, cos, zeros], axis=1), jnp.concatenate([-sin, sin, zeros], axis=1))


def _rope_perm():
    p = np.zeros((2 * LANES, HEADS * LANES), np.float32)
    for h in range(HEADS):
        for d in range(HALF_ROPE):
            p[h * HALF_ROPE + d, h * LANES + d] = 1.0
            p[LANES + h * HALF_ROPE + d, h * LANES + HALF_ROPE + d] = 1.0
    return jnp.asarray(p, BF16)


def _pack_weights(norm_w, w_in, mla_q_a_norm_w, mla_w_uq, mla_q_head_norm_w, mla_w_uk,
                  mla_kv_a_norm_w, mla_k_rope_norm_w):
    kr0 = Q_RANK + KV_RANK
    wa = w_in[:, :kr0].astype(BF16)
    wb = w_in[:, kr0 + QK_ROPE:].astype(BF16)
    wc = jnp.pad(w_in[:, kr0:kr0 + QK_ROPE], ((0, 0), (0, LANES - QK_ROPE))).astype(BF16)
    assert wa.shape[1] == C_SBQ - C_QA and wb.shape[1] == C_KR - C_SBQ
    wuq = mla_w_uq.reshape(Q_RANK, HEADS, Q_HEAD)
    wuq = jnp.concatenate([wuq[:, :, :QK_NOPE].reshape(Q_RANK, -1),
                           wuq[:, :, QK_NOPE:QK_NOPE + HALF_ROPE].reshape(Q_RANK, -1),
                           wuq[:, :, QK_NOPE + HALF_ROPE:].reshape(Q_RANK, -1)], axis=1).astype(BF16)
    g = mla_q_head_norm_w
    gq = jnp.concatenate([jnp.tile(g[:QK_NOPE], HEADS), jnp.tile(g[QK_NOPE:QK_NOPE + HALF_ROPE], HEADS),
                          jnp.tile(g[QK_NOPE + HALF_ROPE:], HEADS)])[None, :]
    wuk_t = jnp.transpose(mla_w_uk, (1, 2, 0))
    zeros = jnp.zeros_like(wuk_t)
    odd = (jnp.arange(HEADS) % 2 == 1)[:, None, None]
    wuk = jnp.concatenate([jnp.where(odd, zeros, wuk_t), jnp.where(odd, wuk_t, zeros)], axis=1).astype(BF16)
    krnw = jnp.concatenate([mla_k_rope_norm_w, jnp.zeros((LANES - QK_ROPE,), F32)])[None, :]
    return [norm_w[None, :], wa, wb, wc, mla_q_a_norm_w[None, :], wuq, gq, wuk, _rope_perm(),
            mla_kv_a_norm_w[None, :], krnw]


def _pack_wuv(mla_w_uv):
    tiled = jnp.tile(jnp.transpose(mla_w_uv, (1, 0, 2)), (1, 1, HEADS))
    own = (jnp.arange(WIDTH) // V_HEAD)[None, None, :] == jnp.arange(HEADS)[:, None, None]
    return jnp.where(own, tiled, 0.0).astype(BF16)


def _tile(n, cap):
    t = cap
    while n % t:
        t //= 2
    return t


def kernel(x_prompt, x_sample, cache_mla_kv, cache_sb_k, cache_sb_v, page_table, norm_w, w_in,
           mla_q_a_norm_w, mla_w_uq, mla_q_head_norm_w, mla_w_uk, mla_kv_a_norm_w, mla_k_rope_norm_w,
           mla_w_uv, w_o_mla, w_o_sb, w_out):
    batch, seq, _ = x_prompt.shape
    n_seq, dec, _ = x_sample.shape
    n_pages = page_table.shape[1]
    past = n_pages * PAGE
    assert n_pages > SB_PREFETCH and dec == 8 and HEADS == 8 and seq % 512 == 0

    weights = _pack_weights(norm_w, w_in, mla_q_a_norm_w, mla_w_uq, mla_q_head_norm_w, mla_w_uk,
                            mla_kv_a_norm_w, mla_k_rope_norm_w)
    wuv = _pack_wuv(mla_w_uv)
    womla, wosb, wout = w_o_mla.astype(BF16), w_o_sb.astype(BF16), w_out.astype(BF16)

    tq = _tile(seq, 256)
    tm = _tile(seq, 256)
    xp = x_prompt.reshape(batch * seq, D_MODEL)
    tabs = _rope_tables(0, seq)
    ntile = seq // tm
    (mlakv_p, sbk_p, sbv_p, qmla, kvbf, sbq, sbkbf, sbvbf, gmla, gsb, mmla, msb) = _proj(
        xp, tabs, lambda i: (i % ntile, 0), weights, tm, groups=batch)
    tri = _tri2(tq)
    omla = _mla_prompt(qmla, kvbf, wuv, batch, seq, tq)
    osb = _sb_prompt(sbq, sbkbf, sbvbf, tri, batch, seq, tq)
    y_prompt = _finish(xp, omla, osb, gmla, gsb, mmla, msb, womla, wosb, wout, _tile(seq, 512))

    ns = n_seq * dec
    tms = _tile(ns, 256)
    xs = x_sample.reshape(ns, D_MODEL)
    tabs_s = [jnp.tile(t, (tms // dec, 1)) for t in _rope_tables(past, dec)]
    (mlakv_s, sbk_s, sbv_s, qmla_s, _, sbq_s, _, _, gmla_s, gsb_s, mmla_s, msb_s) = _proj(
        xs, tabs_s, lambda i: (0, 0), weights, tms)
    tri_s = _tri2(SB_AHEAD + PAGE)
    per_seq = lambda a: a.reshape(n_seq, dec, a.shape[-1])
    pool = cache_mla_kv.shape[0]
    cache_mla_t = jnp.transpose(cache_mla_kv, (0, 2, 1))
    cache_sbk_t = jnp.transpose(cache_sb_k, (0, 2, 3, 1)).reshape(pool, WIDTH, PAGE)
    cache_sbv_t = jnp.transpose(cache_sb_v, (0, 2, 3, 1)).reshape(pool, WIDTH, PAGE)
    olat_s, osb_s = _decode(page_table, per_seq(qmla_s), per_seq(mlakv_s), per_seq(sbq_s),
                            per_seq(sbk_s), per_seq(sbv_s), tri_s, _tri2(PAGE), cache_mla_t, cache_sbk_t, cache_sbv_t)
    omla_s = _uv(olat_s, wuv, tms)
    y_sample = _finish(xs, omla_s, osb_s.reshape(ns, WIDTH), gmla_s, gsb_s, mmla_s, msb_s,
                       womla, wosb, wout, tms)

    heads = lambda a, b, t: a.reshape(b, t, HEADS, SB_DIM)
    heads_t = lambda a: jnp.transpose(a.reshape(batch, HEADS, SB_DIM, seq), (0, 3, 1, 2))
    return (y_prompt.reshape(batch, seq, D_MODEL), y_sample.reshape(n_seq, dec, D_MODEL),
            jnp.transpose(mlakv_p, (0, 2, 1)), heads_t(sbk_p), heads_t(sbv_p),
            mlakv_s.reshape(n_seq, dec, MLA_ROW), heads(sbk_s, n_seq, dec), heads(sbv_s, n_seq, dec))
```
